```python
import jax, jax.numpy as jnp
from jax import lax
import numpy as np

D_MODEL = 2048
BATCH = 4
SEQ = 8192
DEPTH = 1

GRID_W = 64
CTX_LEN = 256
RMS_EPS = 1e-6
N_MOD = 6
MLA_HEADS = 8
Q_LORA = 512
KV_LORA = 256
QK_NOPE = 128
QK_ROPE = 64
V_HEAD = 128
MLA_WIDTH = MLA_HEADS * V_HEAD
ROPE_THETA = 10000.0
Q_BLOCK = 128
HG_HEADS = 8
HG_FDIM = 128
HG_IDIM = 128
HG_FW = HG_HEADS * HG_FDIM
HG_IW = HG_HEADS * HG_IDIM
HG_CHUNK = 64
D_FF = -(-(8 * D_MODEL) // (3 * 256)) * 256
IN_WIDTHS = (Q_LORA, KV_LORA, QK_ROPE, HG_FW, HG_FW, HG_FW, HG_IW, HG_IW, 2 * D_MODEL)
N_IN = sum(IN_WIDTHS)

kernel_name = "hybrid_mla_hgrn2_dit_block"


def rmsnorm(x, g):
    xf = x.astype(jnp.float32)
    y = xf * lax.rsqrt(jnp.mean(xf * xf, axis=-1, keepdims=True) + RMS_EPS)
    return (y * g.astype(jnp.float32)).astype(x.dtype)


def modulate(h, shift, scale):
    return h * (1 + scale) + shift


def axial_rope_tables(rows):
    pairs = QK_ROPE // 4
    row = jnp.repeat(jnp.arange(rows, dtype=jnp.float32), GRID_W)
    col = jnp.tile(jnp.arange(GRID_W, dtype=jnp.float32), rows)
    inv = ROPE_THETA ** (-jnp.arange(pairs, dtype=jnp.float32) / pairs)
    ang = jnp.concatenate([row[:, None] * inv, col[:, None] * inv], axis=-1)
    return jnp.cos(ang)[:, None, :], jnp.sin(ang)[:, None, :]


def apply_rope(x, cos, sin):
    half = QK_ROPE // 2
    xf = x.astype(jnp.float32)
    x1, x2 = xf[..., :half], xf[..., half:]
    out = jnp.concatenate([x1 * cos - x2 * sin, x2 * cos + x1 * sin], axis=-1)
    return out.astype(x.dtype)


def forget_gate(f_logit, lb):
    f = lb + (1.0 - lb) * jax.nn.sigmoid(f_logit.astype(jnp.float32))
    return 1.0 - f, jnp.log(f)


def mixer_projections(h, w_in_l, q_norm_g, kv_norm_g, w_uq_l, w_ukv_l, lb_l, rope):
    bsz, length, _ = h.shape
    p = h @ w_in_l
    c_q, c_kv, k_r, hg_q, hg_ff, hg_fb, hg_i, hg_g, gate_logits = jnp.split(
        p, np.cumsum(IN_WIDTHS)[:-1].tolist(), axis=-1)
    q = (rmsnorm(c_q, q_norm_g) @ w_uq_l).reshape(bsz, length, MLA_HEADS, QK_NOPE + QK_ROPE)
    kv = (rmsnorm(c_kv, kv_norm_g) @ w_ukv_l).reshape(bsz, length, MLA_HEADS, QK_NOPE + V_HEAD)
    q_nope, q_rope = q[..., :QK_NOPE], q[..., QK_NOPE:]
    k_nope, v = kv[..., :QK_NOPE], kv[..., QK_NOPE:]
    k_rope = k_r[:, :, None, :]
    if rope is not None:
        cos, sin = rope
        q_rope = apply_rope(q_rope, cos, sin)
        k_rope = apply_rope(k_rope, cos, sin)
    q = jnp.concatenate([q_nope, q_rope], axis=-1)
    k = jnp.concatenate([k_nope, jnp.broadcast_to(k_rope, (bsz, length, MLA_HEADS, QK_ROPE))], axis=-1)
    heads = lambda t: t.reshape(bsz, length, HG_HEADS, -1)
    hq = heads(jax.nn.silu(hg_q)) * (HG_FDIM ** -0.5)
    k_f, lf_f = forget_gate(heads(hg_ff), lb_l[0])
    k_b, lf_b = forget_gate(heads(hg_fb), lb_l[1])
    hv = heads(hg_i)
    hgate = heads(hg_g)
    return (q, k, v), (hq, k_f, lf_f, k_b, lf_b, hv, hgate), gate_logits.reshape(bsz, length, 2, D_MODEL)


def block_attention(q, k, v):
    bsz, length, nh, dqk = q.shape
    nb = length // Q_BLOCK
    scale = dqk ** -0.5
    qb = q.reshape(bsz, nb, Q_BLOCK, nh, dqk).swapaxes(0, 1)

    def one_block(qblk):
        s = jnp.einsum('bqhd,bkhd->bhqk', qblk, k).astype(jnp.float32) * scale
        p = jax.nn.softmax(s, axis=-1)
        return jnp.einsum('bhqk,bkhv->bqhv', p.astype(v.dtype), v)

    out = lax.map(one_block, qb)
    return out.swapaxes(0, 1).reshape(bsz, length, nh, v.shape[-1])


def gla_chunkwise(q, k, v, log_f, s0):
    out_dtype = v.dtype
    bsz, length, nh, dk = q.shape
    dv = v.shape[-1]
    n = length // HG_CHUNK
    q, k, v, log_f = [t.astype(jnp.float32).reshape(bsz, n, HG_CHUNK, nh, -1) for t in (q, k, v, log_f)]
    b = jnp.cumsum(log_f, axis=2)
    b_last = b[:, :, -1]
    q_d = q * jnp.exp(b)
    k_d = k * jnp.exp(-b)
    k_tail = k * jnp.exp(b_last[:, :, None] - b)
    mask = jnp.tril(jnp.ones((HG_CHUNK, HG_CHUNK), dtype=bool))
    scores = jnp.where(mask, jnp.einsum('bnthd,bnshd->bnhts', q_d, k_d), 0.0)
    o_intra = jnp.einsum('bnhts,bnshv->bnthv', scores, v)
    u = jnp.einsum('bnshd,bnshv->bnhdv', k_tail, v)
    decay = jnp.exp(b_last)

    def step(s, inp):
        d, u_j = inp
        return d[..., None] * s + u_j, s

    s_final, s_starts = lax.scan(step, s0.astype(jnp.float32),
                                 (jnp.moveaxis(decay, 1, 0), jnp.moveaxis(u, 1, 0)))
    s_starts = jnp.moveaxis(s_starts, 0, 1)
    o_inter = jnp.einsum('bnthd,bnhdv->bnthv', q_d, s_starts)
    o = (o_intra + o_inter).reshape(bsz, length, nh, dv)
    return o.astype(out_dtype), s_final


def hgrn2_bidirectional(hq, k_f, lf_f, k_b, lf_b, hv, s_f0, s_b0):
    o_f, s_f = gla_chunkwise(hq, k_f, hv, lf_f, s_f0)
    rev = lambda t: jnp.flip(t, axis=1)
    o_b, s_b = gla_chunkwise(rev(hq), rev(k_b), rev(hv), rev(lf_b), s_b0)
    return o_f + rev(o_b), s_f, s_b


def merge_branches(attn_o, hg_o, hgate, gate_logits, o_norm_g, w_br_mla_l, w_br_hgrn_l, w_out_l):
    bsz, length = attn_o.shape[:2]
    hg_o = rmsnorm(hg_o, o_norm_g) * jax.nn.silu(hgate)
    y_a = attn_o.reshape(bsz, length, MLA_WIDTH) @ w_br_mla_l
    y_h = hg_o.reshape(bsz, length, HG_IW) @ w_br_hgrn_l
    g = jax.nn.sigmoid(gate_logits)
    return (g[..., 0, :] * y_a + g[..., 1, :] * y_h) @ w_out_l


def swiglu(h, w_in_l, w_out_l):
    a, b = jnp.split(h @ w_in_l, 2, axis=-1)
    return (jax.nn.silu(a) * b) @ w_out_l


def setup_inputs(seed: int = 0) -> dict:
    key = jax.random.key(seed)
    ks = jax.random.split(key, 19)
    f32 = jnp.float32
    nrm = lambda k, shape, fan_in: jax.random.normal(k, shape, f32) * (fan_in ** -0.5)
    gain = lambda k, shape: 1.0 + 0.05 * jax.random.normal(k, shape, f32)
    return {
        "x": jax.random.normal(ks[0], (BATCH, SEQ, D_MODEL), f32),
        "c": jax.random.normal(ks[1], (BATCH, D_MODEL), f32),
        "ctx": jax.random.normal(ks[2], (BATCH, CTX_LEN, D_MODEL), f32),
        "c_ctx": jax.random.normal(ks[3], (D_MODEL,), f32),
        "w_mod": nrm(ks[4], (DEPTH, D_MODEL, N_MOD * D_MODEL), D_MODEL),
        "b_mod": 0.02 * jax.random.normal(ks[5], (DEPTH, N_MOD * D_MODEL), f32),
        "norm_g": gain(ks[6], (DEPTH, 4, D_MODEL)),
        "w_in": nrm(ks[7], (DEPTH, D_MODEL, N_IN), D_MODEL),
        "mla_q_norm": gain(ks[8], (DEPTH, Q_LORA)),
        "mla_kv_norm": gain(ks[9], (DEPTH, KV_LORA)),
        "w_uq": nrm(ks[10], (DEPTH, Q_LORA, MLA_HEADS * (QK_NOPE + QK_ROPE)), Q_LORA),
        "w_ukv": nrm(ks[11], (DEPTH, KV_LORA, MLA_HEADS * (QK_NOPE + V_HEAD)), KV_LORA),
        "hgrn_lb": 0.1 * jax.random.normal(ks[12], (DEPTH + 1, 2, HG_HEADS, HG_FDIM), f32),
        "hgrn_o_norm": gain(ks[13], (DEPTH, HG_IDIM)),
        "w_br_mla": nrm(ks[14], (DEPTH, MLA_WIDTH, D_MODEL), MLA_WIDTH),
        "w_br_hgrn": nrm(ks[15], (DEPTH, HG_IW, D_MODEL), HG_IW),
        "w_out": nrm(ks[16], (DEPTH, D_MODEL, D_MODEL), D_MODEL),
        "w_ffn_in": nrm(ks[17], (DEPTH, D_MODEL, 2 * D_FF), D_MODEL),
        "w_ffn_out": nrm(ks[18], (DEPTH, D_FF, D_MODEL), D_FF),
    }


def reference(x, c, ctx, c_ctx, w_mod, b_mod, norm_g, w_in, mla_q_norm, mla_kv_norm, w_uq, w_ukv,
              hgrn_lb, hgrn_o_norm, w_br_mla, w_br_hgrn, w_out, w_ffn_in, w_ffn_out):
    bsz, length, _ = x.shape
    rows = length // GRID_W
    rope = axial_rope_tables(rows)
    lb_all = jnp.cumsum(jax.nn.softmax(hgrn_lb.astype(jnp.float32), axis=0), axis=0)
    zero_state = jnp.zeros((bsz, HG_HEADS, HG_FDIM, HG_IDIM), jnp.float32)
    for layer in range(DEPTH):
        last = layer == DEPTH - 1
        mod = jax.nn.silu(c) @ w_mod[layer] + b_mod[layer]
        mod_c = jax.nn.silu(c_ctx) @ w_mod[layer] + b_mod[layer]
        sh_a, sc_a, gt_a, sh_f, sc_f, gt_f = [t[:, None, :] for t in jnp.split(mod, N_MOD, axis=-1)]
        csh_a, csc_a, cgt_a, csh_f, csc_f, cgt_f = jnp.split(mod_c, N_MOD, axis=-1)

        h = modulate(rmsnorm(x, norm_g[layer, 0]), sh_a, sc_a)
        hc = modulate(rmsnorm(ctx, norm_g[layer, 0]), csh_a, csc_a)
        (q_l, k_l, v_l), hg_l, gl_l = mixer_projections(
            h, w_in[layer], mla_q_norm[layer], mla_kv_norm[layer], w_uq[layer], w_ukv[layer], lb_all[layer], rope)
        (q_c, k_c, v_c), hg_c, gl_c = mixer_projections(
            hc, w_in[layer], mla_q_norm[layer], mla_kv_norm[layer], w_uq[layer], w_ukv[layer], lb_all[layer], None)
        attn_l = block_attention(q_l, jnp.concatenate([k_c, k_l], axis=1), jnp.concatenate([v_c, v_l], axis=1))
        hgo_c, s_f, s_b = hgrn2_bidirectional(*hg_c[:6], zero_state, zero_state)
        hgo_l, _, _ = hgrn2_bidirectional(*hg_l[:6], s_f, s_b)
        out_l = merge_branches(attn_l, hgo_l, hg_l[6], gl_l, hgrn_o_norm[layer],
                               w_br_mla[layer], w_br_hgrn[layer], w_out[layer])
        x = x + gt_a * rmsnorm(out_l, norm_g[layer, 1])
        h = modulate(rmsnorm(x, norm_g[layer, 2]), sh_f, sc_f)
        x = x + gt_f * rmsnorm(swiglu(h, w_ffn_in[layer], w_ffn_out[layer]), norm_g[layer, 3])

        if not last:
            attn_c = block_attention(q_c, k_c, v_c)
            out_c = merge_branches(attn_c, hgo_c, hg_c[6], gl_c, hgrn_o_norm[layer],
                                   w_br_mla[layer], w_br_hgrn[layer], w_out[layer])
            ctx = ctx + cgt_a * rmsnorm(out_c, norm_g[layer, 1])
            hc = modulate(rmsnorm(ctx, norm_g[layer, 2]), csh_f, csc_f)
            ctx = ctx + cgt_f * rmsnorm(swiglu(hc, w_ffn_in[layer], w_ffn_out[layer]), norm_g[layer, 3])
    return x
```

```python
import functools

import jax
import jax.numpy as jnp
from jax import lax
from jax.experimental import pallas as pl
from jax.experimental.pallas import tpu as pltpu

F32 = jnp.float32
BF16 = jnp.bfloat16

RMS_EPS = 1e-6
N_MOD = 6
GRID_W = 64
ROPE_THETA = 10000.0
HEADS = 8
Q_LORA = 512
KV_LORA = 256
QK_NOPE = 128
QK_ROPE = 64
V_HEAD = 128
QK_PAD = 256
HG_DIM = 128
HG_CHUNK = 64
HG_GROUP = 8
LANES = 128
MIB = 1024 * 1024

COL_GATES = 0
COL_HQ = 4096
COL_HI = 5120
COL_HG = 6144
COL_CQ = 7168
COL_CKV = 7680
COL_KR = 7936
N16 = 8192


def _pick(n, candidates):
    for c in candidates:
        if n % c == 0:
            return c
    raise ValueError(f"no tile for {n} in {candidates}")


def _params(sem, vmem_mib):
    return pltpu.CompilerParams(dimension_semantics=sem, vmem_limit_bytes=vmem_mib * MIB)


def _rms(xf):
    return xf * lax.rsqrt(jnp.mean(xf * xf, axis=-1, keepdims=True) + RMS_EPS)


def _silu(x):
    return x * jax.nn.sigmoid(x)


def _mod_kernel(c_ref, w_ref, b_ref, o_ref):
    s = _silu(c_ref[...]).astype(BF16)
    o_ref[...] = jnp.dot(s, w_ref[...].astype(BF16), preferred_element_type=F32) + b_ref[...]


def _mod(cc, w_mod, b_mod):
    d, n = w_mod.shape
    tn = _pick(n, (1024, 512, 256, 128))
    return pl.pallas_call(
        _mod_kernel,
        grid=(n // tn,),
        in_specs=[pl.BlockSpec((8, d), lambda j: (0, 0)),
                  pl.BlockSpec((d, tn), lambda j: (0, j)),
                  pl.BlockSpec((1, tn), lambda j: (0, j))],
        out_specs=pl.BlockSpec((8, tn), lambda j: (0, j)),
        out_shape=jax.ShapeDtypeStruct((8, n), F32),
        compiler_params=_params(("arbitrary",), 40),
        name="mod",
    )(cc, w_mod, b_mod)


def _norm_mod_kernel(x_ref, ctx_ref, mod_ref, g_ref, o_ref, *, d, ctx_row):
    b = pl.program_id(0)
    t = pl.program_id(1)

    def emit(src_ref, row):
        y = _rms(src_ref[...]) * g_ref[0:1, :]
        sh = mod_ref[pl.ds(row, 1), 0:d]
        sc = mod_ref[pl.ds(row, 1), d:2 * d]
        o_ref[...] = (y * (1.0 + sc) + sh).astype(BF16)

    @pl.when(t == 0)
    def _():
        emit(ctx_ref, ctx_row)

    @pl.when(t > 0)
    def _():
        emit(x_ref, b)


def _norm_mod(x, ctx, mod, norm_g):
    bsz, length, d = x.shape
    ctx_len = ctx.shape[1]
    nt = length // ctx_len
    return pl.pallas_call(
        functools.partial(_norm_mod_kernel, d=d, ctx_row=bsz),
        grid=(bsz, nt + 1),
        in_specs=[pl.BlockSpec((None, ctx_len, d), lambda b, t: (b, jnp.maximum(t - 1, 0), 0)),
                  pl.BlockSpec((None, ctx_len, d), lambda b, t: (b, 0, 0)),
                  pl.BlockSpec(mod.shape, lambda b, t: (0, 0)),
                  pl.BlockSpec(norm_g.shape, lambda b, t: (0, 0))],
        out_specs=pl.BlockSpec((None, ctx_len, d), lambda b, t: (b, t, 0)),
        out_shape=jax.ShapeDtypeStruct((bsz, length + ctx_len, d), BF16),
        compiler_params=_params(("arbitrary", "arbitrary"), 32),
        name="norm_mod",
    )(x, ctx, mod, norm_g)


def _matmul_kernel(a_ref, w_ref, o_ref):
    o_ref[...] = jnp.dot(a_ref[...], w_ref[...], preferred_element_type=F32).astype(o_ref.dtype)


def _matmul(a, w, out_dtype, name):
    m, k = a.shape
    n = w.shape[1]
    tm = _pick(m, (1024, 512, 256))
    tn = _pick(n, (1024, 512, 256))
    return pl.pallas_call(
        _matmul_kernel,
        grid=(m // tm, n // tn),
        in_specs=[pl.BlockSpec((tm, k), lambda i, j: (i, 0)),
                  pl.BlockSpec((k, tn), lambda i, j: (0, j))],
        out_specs=pl.BlockSpec((tm, tn), lambda i, j: (i, j)),
        out_shape=jax.ShapeDtypeStruct((m, n), out_dtype),
        compiler_params=_params(("arbitrary", "arbitrary"), 40),
        name=name,
    )(a, w)


def _mla_prep_kernel(cq_ref, ckv_ref, kr_ref, cos_ref, sin_ref, gq_ref, gkv_ref, wqm_ref, wqr_ref, wkv_ref,
                     q_ref, k_ref, v_ref, *, q_scale):
    cos = cos_ref[...]
    sin = sin_ref[...]
    cqn = (_rms(cq_ref[...].astype(F32)) * gq_ref[...]).astype(BF16)
    main = jnp.dot(cqn, wqm_ref[...], preferred_element_type=F32)
    rot = jnp.dot(cqn, wqr_ref[...], preferred_element_type=F32)
    ckvn = (_rms(ckv_ref[...].astype(F32)) * gkv_ref[...]).astype(BF16)
    kv = jnp.dot(ckvn, wkv_ref[...], preferred_element_type=F32)
    lane = lax.broadcasted_iota(jnp.int32, cos.shape, 1)
    a = kr_ref[...].astype(F32) * jnp.where(lane < QK_ROPE, cos, sin)
    k_rope = (a + pltpu.roll(a, QK_ROPE, axis=1)).astype(BF16)
    for h in range(HEADS):
        c0 = h * QK_PAD
        q_ref[:, c0:c0 + QK_NOPE] = (main[:, c0:c0 + QK_NOPE] * q_scale).astype(BF16)
        q_rope = main[:, c0 + QK_NOPE:c0 + QK_PAD] * cos + rot[:, h * LANES:(h + 1) * LANES] * sin
        q_ref[:, c0 + QK_NOPE:c0 + QK_PAD] = (q_rope * q_scale).astype(BF16)
        k_ref[:, c0:c0 + QK_NOPE] = kv[:, h * QK_NOPE:(h + 1) * QK_NOPE].astype(BF16)
        k_ref[:, c0 + QK_NOPE:c0 + QK_PAD] = k_rope
    v_ref[...] = kv[:, HEADS * QK_NOPE:].astype(BF16)


def _mla_prep(p16, cos_t, sin_t, gq, gkv, wq_main, wq_rot, wkv, ctx_len):
    bsz, lk, _ = p16.shape
    tm = ctx_len
    nt = lk // tm
    whole = lambda arr: pl.BlockSpec(arr.shape, lambda b, t: (0,) * arr.ndim)
    return pl.pallas_call(
        functools.partial(_mla_prep_kernel, q_scale=float((QK_NOPE + QK_ROPE) ** -0.5)),
        grid=(bsz, nt),
        in_specs=[pl.BlockSpec((None, tm, Q_LORA), lambda b, t: (b, t, COL_CQ // Q_LORA)),
                  pl.BlockSpec((None, tm, KV_LORA), lambda b, t: (b, t, COL_CKV // KV_LORA)),
                  pl.BlockSpec((None, tm, LANES), lambda b, t: (b, t, COL_KR // LANES)),
                  pl.BlockSpec((tm, LANES), lambda b, t: (t, 0)),
                  pl.BlockSpec((tm, LANES), lambda b, t: (t, 0)),
                  whole(gq), whole(gkv), whole(wq_main), whole(wq_rot), whole(wkv)],
        out_specs=[pl.BlockSpec((None, tm, HEADS * QK_PAD), lambda b, t: (b, jnp.maximum(t - 1, 0), 0)),
                   pl.BlockSpec((None, tm, HEADS * QK_PAD), lambda b, t: (b, t, 0)),
                   pl.BlockSpec((None, tm, HEADS * V_HEAD), lambda b, t: (b, t, 0))],
        out_shape=[jax.ShapeDtypeStruct((bsz, lk - ctx_len, HEADS * QK_PAD), BF16),
                   jax.ShapeDtypeStruct((bsz, lk, HEADS * QK_PAD), BF16),
                   jax.ShapeDtypeStruct((bsz, lk, HEADS * V_HEAD), BF16)],
        compiler_params=_params(("arbitrary", "arbitrary"), 40),
        name="mla_prep",
    )(p16, p16, p16, cos_t, sin_t, gq, gkv, wq_main, wq_rot, wkv)


def _attn_kernel(q_ref, k_ref, v_ref, o_ref, m_ref, l_ref, acc_ref, *, tk, nk):
    q = q_ref[...]
    m_ref[...] = jnp.full(m_ref.shape, -jnp.inf, F32)
    l_ref[...] = jnp.zeros(l_ref.shape, F32)
    acc_ref[...] = jnp.zeros(acc_ref.shape, F32)

    def body(j, carry):
        r0 = pl.multiple_of(j * tk, tk)
        ks = k_ref[pl.ds(r0, tk), :]
        vs = v_ref[pl.ds(r0, tk), :]
        s = lax.dot_general(q, ks, (((1,), (1,)), ((), ())), preferred_element_type=F32)
        m_old = m_ref[...]
        m_new = jnp.maximum(m_old, jnp.max(s, axis=-1, keepdims=True))
        p = jnp.exp(s - m_new)
        alpha = jnp.exp(m_old - m_new)
        l_ref[...] = alpha * l_ref[...] + jnp.sum(p, axis=-1, keepdims=True)
        acc_ref[...] = alpha * acc_ref[...] + jnp.dot(p.astype(BF16), vs, preferred_element_type=F32)
        m_ref[...] = m_new
        return carry

    lax.fori_loop(0, nk, body, 0)
    o_ref[...] = (acc_ref[...] / l_ref[...]).astype(o_ref.dtype)


def _attention(q, k, v):
    bsz, length, _ = q.shape
    lk = k.shape[1]
    tq = _pick(length, (512, 256))
    tk = _pick(lk, (768, 640, 512, 256))
    return pl.pallas_call(
        functools.partial(_attn_kernel, tk=tk, nk=lk // tk),
        grid=(bsz, HEADS, length // tq),
        in_specs=[pl.BlockSpec((None, tq, QK_PAD), lambda b, h, i: (b, i, h)),
                  pl.BlockSpec((None, lk, QK_PAD), lambda b, h, i: (b, 0, h)),
                  pl.BlockSpec((None, lk, V_HEAD), lambda b, h, i: (b, 0, h))],
        out_specs=pl.BlockSpec((None, tq, V_HEAD), lambda b, h, i: (b, i, h)),
        out_shape=jax.ShapeDtypeStruct((bsz, length, HEADS * V_HEAD), BF16),
        scratch_shapes=[pltpu.VMEM((tq, 1), F32), pltpu.VMEM((tq, 1), F32), pltpu.VMEM((tq, V_HEAD), F32)],
        compiler_params=_params(("arbitrary", "arbitrary", "arbitrary"), 40),
        name="attn",
    )(q, k, v)


def _hgrn_kernel(hq_ref, hi_ref, hg_ref, ff_ref, fb_ref, lb_ref, gn_ref, o_ref, oacc_ref, s_ref,
                 *, ctx_len, length):
    c = HG_CHUNK
    gsz = HG_GROUP * c
    n_groups = length // gsz
    n_ctx_chunks = ctx_len // c
    row = lax.broadcasted_iota(jnp.int32, (c, c), 0)
    col = lax.broadcasted_iota(jnp.int32, (c, c), 1)
    tri_f = row >= col
    tri_b = row <= col
    lbr = lb_ref[...]
    q_scale = float(HG_DIM ** -0.5)

    def lower_bound(direction):
        a0 = lbr[direction:direction + 1, :]
        a1 = lbr[2 + direction:3 + direction, :]
        mx = jnp.maximum(a0, a1)
        e0 = jnp.exp(a0 - mx)
        e1 = jnp.exp(a1 - mx)
        return e0 / (e0 + e1)

    def chunk(direction, f_ref, lb, r0, s, want_out):
        tri = tri_f if direction == 0 else tri_b
        f = lb + (1.0 - lb) * jax.nn.sigmoid(f_ref[pl.ds(r0, c), :])
        kk = 1.0 - f
        lf = jnp.log(f)
        lf_hi = lf.astype(BF16)
        lf_lo = (lf - lf_hi.astype(F32)).astype(BF16)
        tri_m = tri.astype(BF16)
        bcum = (jnp.dot(tri_m, lf_hi, preferred_element_type=F32)
                + jnp.dot(tri_m, lf_lo, preferred_element_type=F32))
        b_last = bcum[c - 1:c, :] if direction == 0 else bcum[0:1, :]
        v = hi_ref[pl.ds(r0, c), :]
        k_tail = (kk * jnp.exp(b_last - bcum)).astype(BF16)
        u = lax.dot_general(k_tail, v, (((0,), (0,)), ((), ())), preferred_element_type=F32)
        decay = jnp.transpose(jnp.broadcast_to(jnp.exp(b_last), (HG_DIM, HG_DIM)))
        out = None
        if want_out:
            q = _silu(hq_ref[pl.ds(r0, c), :].astype(F32)) * q_scale
            q_d = (q * jnp.exp(bcum)).astype(BF16)
            k_d = (kk * jnp.exp(-bcum)).astype(BF16)
            sc = lax.dot_general(q_d, k_d, (((1,), (1,)), ((), ())), preferred_element_type=F32)
            sc = jnp.where(tri, sc, 0.0).astype(BF16)
            out = (jnp.dot(sc, v, preferred_element_type=F32)
                   + jnp.dot(q_d, s.astype(BF16), preferred_element_type=F32))
        return out, decay * s + u

    def run(direction):
        f_ref = ff_ref if direction == 0 else fb_ref
        lb = lower_bound(direction)
        s = jnp.zeros((HG_DIM, HG_DIM), F32)
        order = (lambda n: range(n)) if direction == 0 else (lambda n: range(n - 1, -1, -1))
        for j in order(n_ctx_chunks):
            _, s = chunk(direction, f_ref, lb, j * c, s, False)
        s_ref[...] = s

        def group(gi, carry):
            g = gi if direction == 0 else n_groups - 1 - gi
            s = s_ref[...]
            for j in order(HG_GROUP):
                r0 = pl.multiple_of(ctx_len + g * gsz + j * c, c)
                o0 = pl.multiple_of(g * gsz + j * c, c)
                out, s = chunk(direction, f_ref, lb, r0, s, True)
                if direction == 0:
                    oacc_ref[pl.ds(o0, c), :] = out
                else:
                    tot = oacc_ref[pl.ds(o0, c), :] + out
                    gate = _silu(hg_ref[pl.ds(r0, c), :].astype(F32))
                    o_ref[pl.ds(o0, c), :] = (_rms(tot) * gn_ref[...] * gate).astype(o_ref.dtype)
            s_ref[...] = s
            return carry

        lax.fori_loop(0, n_groups, group, 0)

    run(0)
    run(1)


def _hgrn(p16, p32, lb_rows, gn, ctx_len):
    bsz, lk, _ = p16.shape
    length = lk - ctx_len
    strip = lambda col: pl.BlockSpec((None, lk, LANES), lambda b, h: (b, 0, col // LANES + h))
    return pl.pallas_call(
        functools.partial(_hgrn_kernel, ctx_len=ctx_len, length=length),
        grid=(bsz, HEADS),
        in_specs=[strip(COL_HQ), strip(COL_HI), strip(COL_HG), strip(0), strip(HEADS * HG_DIM),
                  pl.BlockSpec((None, 4, LANES), lambda b, h: (h, 0, 0)),
                  pl.BlockSpec(gn.shape, lambda b, h: (0, 0))],
        out_specs=pl.BlockSpec((None, length, LANES), lambda b, h: (b, 0, h)),
        out_shape=jax.ShapeDtypeStruct((bsz, length, HEADS * HG_DIM), BF16),
        scratch_shapes=[pltpu.VMEM((length, HG_DIM), F32), pltpu.VMEM((HG_DIM, HG_DIM), F32)],
        compiler_params=_params(("arbitrary", "arbitrary"), 48),
        name="hgrn",
    )(p16, p16, p16, p32, p32, lb_rows, gn)


def _merge_kernel(attn_ref, hg_ref, gl_ref, x_ref, mod_ref, ng_ref, wa_ref, wh_ref, wo_ref, x1_ref, h2_ref, *, d):
    b = pl.program_id(0)
    ya = jnp.dot(attn_ref[...], wa_ref[...], preferred_element_type=F32)
    yh = jnp.dot(hg_ref[...], wh_ref[...], preferred_element_type=F32)
    g0 = jax.nn.sigmoid(gl_ref[:, 0:d].astype(F32))
    g1 = jax.nn.sigmoid(gl_ref[:, d:2 * d].astype(F32))
    z = (g0 * ya + g1 * yh).astype(BF16)
    o = jnp.dot(z, wo_ref[...], preferred_element_type=F32)
    gt_a = mod_ref[pl.ds(b, 1), 2 * d:3 * d]
    sh_f = mod_ref[pl.ds(b, 1), 3 * d:4 * d]
    sc_f = mod_ref[pl.ds(b, 1), 4 * d:5 * d]
    x1 = x_ref[...] + gt_a * (_rms(o) * ng_ref[1:2, :])
    x1_ref[...] = x1
    h2_ref[...] = (_rms(x1) * ng_ref[2:3, :] * (1.0 + sc_f) + sh_f).astype(BF16)


def _merge(attn, hgn, p16, x, mod, norm_g, w_bm, w_bh, w_o, ctx_len):
    bsz, length, d = x.shape
    tm = ctx_len
    off = ctx_len // tm
    whole = lambda arr: pl.BlockSpec(arr.shape, lambda b, i: (0,) * arr.ndim)
    return pl.pallas_call(
        functools.partial(_merge_kernel, d=d),
        grid=(bsz, length // tm),
        in_specs=[pl.BlockSpec((None, tm, attn.shape[2]), lambda b, i: (b, i, 0)),
                  pl.BlockSpec((None, tm, hgn.shape[2]), lambda b, i: (b, i, 0)),
                  pl.BlockSpec((None, tm, 2 * d), lambda b, i: (b, i + off, COL_GATES // (2 * d))),
                  pl.BlockSpec((None, tm, d), lambda b, i: (b, i, 0)),
                  whole(mod), whole(norm_g), whole(w_bm), whole(w_bh), whole(w_o)],
        out_specs=[pl.BlockSpec((None, tm, d), lambda b, i: (b, i, 0)),
                   pl.BlockSpec((None, tm, d), lambda b, i: (b, i, 0))],
        out_shape=[jax.ShapeDtypeStruct((bsz, length, d), F32),
                   jax.ShapeDtypeStruct((bsz, length, d), BF16)],
        compiler_params=_params(("arbitrary", "arbitrary"), 56),
        name="merge",
    )(attn, hgn, p16, x, mod, norm_g, w_bm, w_bh, w_o)


def _ffn_kernel(h_ref, wa_ref, wb_ref, wo_ref, x1_ref, mod_ref, ng_ref, o_ref, acc_ref, *, d, nf):
    b = pl.program_id(0)
    f = pl.program_id(2)

    @pl.when(f == 0)
    def _():
        acc_ref[...] = jnp.zeros(acc_ref.shape, F32)

    h = h_ref[...]
    a = jnp.dot(h, wa_ref[...], preferred_element_type=F32)
    g = jnp.dot(h, wb_ref[...], preferred_element_type=F32)
    act = (_silu(a) * g).astype(BF16)
    acc_ref[...] += jnp.dot(act, wo_ref[...], preferred_element_type=F32)

    @pl.when(f == nf - 1)
    def _():
        gt_f = mod_ref[pl.ds(b, 1), 5 * d:6 * d]
        o_ref[...] = x1_ref[...] + gt_f * (_rms(acc_ref[...]) * ng_ref[3:4, :])


def _ffn(h2, x1, mod, norm_g, w_fi, w_fo):
    bsz, length, d = x1.shape
    d_ff = w_fo.shape[0]
    tm = _pick(length, (512, 256))
    tf = _pick(d_ff, (512, 256, 128))
    nf = d_ff // tf
    whole = lambda arr: pl.BlockSpec(arr.shape, lambda b, i, f: (0,) * arr.ndim)
    return pl.pallas_call(
        functools.partial(_ffn_kernel, d=d, nf=nf),
        grid=(bsz, length // tm, nf),
        in_specs=[pl.BlockSpec((None, tm, d), lambda b, i, f: (b, i, 0)),
                  pl.BlockSpec((d, tf), lambda b, i, f: (0, f)),
                  pl.BlockSpec((d, tf), lambda b, i, f: (0, nf + f)),
                  pl.BlockSpec((tf, d), lambda b, i, f: (f, 0)),
                  pl.BlockSpec((None, tm, d), lambda b, i, f: (b, i, 0)),
                  whole(mod), whole(norm_g)],
        out_specs=pl.BlockSpec((None, tm, d), lambda b, i, f: (b, i, 0)),
        out_shape=jax.ShapeDtypeStruct((bsz, length, d), F32),
        scratch_shapes=[pltpu.VMEM((tm, d), F32)],
        compiler_params=_params(("arbitrary", "arbitrary", "arbitrary"), 48),
        name="ffn",
    )(h2, w_fi, w_fi, w_fo, x1, mod, norm_g)


def _rotate_half_cols(w):
    half = QK_ROPE // 2
    return jnp.concatenate([-w[..., half:], w[..., :half]], axis=-1)


def _rope_tables(length, ctx_len):
    rows = length // GRID_W
    pairs = QK_ROPE // 4
    row = jnp.repeat(jnp.arange(rows, dtype=F32), GRID_W)
    col = jnp.tile(jnp.arange(GRID_W, dtype=F32), rows)
    inv = ROPE_THETA ** (-jnp.arange(pairs, dtype=F32) / pairs)
    ang = jnp.concatenate([row[:, None] * inv, col[:, None] * inv], axis=-1)
    rep = LANES // (QK_ROPE // 2)
    cos = jnp.concatenate([jnp.ones((ctx_len, LANES), F32), jnp.tile(jnp.cos(ang), (1, rep))], axis=0)
    sin = jnp.concatenate([jnp.zeros((ctx_len, LANES), F32), jnp.tile(jnp.sin(ang), (1, rep))], axis=0)
    return cos, sin


def kernel(x, c, ctx, c_ctx, w_mod, b_mod, norm_g, w_in, mla_q_norm, mla_kv_norm, w_uq, w_ukv,
           hgrn_lb, hgrn_o_norm, w_br_mla, w_br_hgrn, w_out, w_ffn_in, w_ffn_out):
    bsz, length, d = x.shape
    ctx_len = ctx.shape[1]
    depth = w_mod.shape[0]
    assert depth == 1 and hgrn_lb.shape[0] == 2, "kernel is written for the depth-1 block"
    assert bsz < 8 and length % (HG_GROUP * HG_CHUNK) == 0 and ctx_len % HG_CHUNK == 0
    lk = length + ctx_len

    w = w_in[0]
    o = [0]
    for width in (Q_LORA, KV_LORA, QK_ROPE, 1024, 1024, 1024, 1024, 1024, 2 * d):
        o.append(o[-1] + width)
    w_cq, w_ckv, w_kr, w_hq, w_ff, w_fb, w_hi, w_hg, w_gl = [w[:, o[i]:o[i + 1]] for i in range(9)]
    w16 = jnp.concatenate([w_gl, w_hq, w_hi, w_hg, w_cq, w_ckv, w_kr, _rotate_half_cols(w_kr),
                           jnp.zeros((d, N16 - COL_KR - 2 * QK_ROPE), F32)], axis=1).astype(BF16)
    w32 = jnp.concatenate([w_ff, w_fb], axis=1).astype(BF16)

    wq = w_uq[0].reshape(Q_LORA, HEADS, QK_NOPE + QK_ROPE)
    wq_nope, wq_rope = wq[..., :QK_NOPE], wq[..., QK_NOPE:]
    zpad = jnp.zeros((Q_LORA, HEADS, QK_PAD - QK_NOPE - QK_ROPE), F32)
    wq_main = jnp.concatenate([wq_nope, wq_rope, zpad], axis=-1).reshape(Q_LORA, HEADS * QK_PAD).astype(BF16)
    wq_rot = jnp.concatenate([_rotate_half_cols(wq_rope), zpad], axis=-1).reshape(Q_LORA, HEADS * LANES).astype(BF16)
    wkv = w_ukv[0].reshape(KV_LORA, HEADS, QK_NOPE + V_HEAD)
    wkv = jnp.concatenate([wkv[..., :QK_NOPE].reshape(KV_LORA, -1), wkv[..., QK_NOPE:].reshape(KV_LORA, -1)],
                          axis=1).astype(BF16)
    w_bm = w_br_mla[0].astype(BF16)
    w_bh = w_br_hgrn[0].astype(BF16)
    w_o = w_out[0].astype(BF16)
    w_fi = w_ffn_in[0].astype(BF16)
    w_fo = w_ffn_out[0].astype(BF16)
    lb_rows = hgrn_lb.transpose(2, 0, 1, 3).reshape(HEADS, 4, HG_DIM)
    cos_t, sin_t = _rope_tables(length, ctx_len)

    cc = jnp.concatenate([c, c_ctx[None, :], jnp.zeros((8 - bsz - 1, d), F32)], axis=0)
    mod = _mod(cc, w_mod[0], b_mod[0][None, :])
    ng = norm_g[0]

    hcat = _norm_mod(x, ctx, mod, ng)
    h2d = hcat.reshape(bsz * lk, d)
    p16 = _matmul(h2d, w16, BF16, "in_proj16").reshape(bsz, lk, N16)
    p32 = _matmul(h2d, w32, F32, "in_proj32").reshape(bsz, lk, 2 * HEADS * HG_DIM)
    q, k, v = _mla_prep(p16, cos_t, sin_t, mla_q_norm[0][None, :], mla_kv_norm[0][None, :],
                        wq_main, wq_rot, wkv, ctx_len)
    attn = _attention(q, k, v)
    hgn = _hgrn(p16, p32, lb_rows, hgrn_o_norm[0][None, :], ctx_len)
    x1, h2 = _merge(attn, hgn, p16, x, mod, ng, w_bm, w_bh, w_o, ctx_len)
    return _ffn(h2, x1, mod, ng, w_fi, w_fo)
```

```python
import functools

import jax
import jax.numpy as jnp
from jax import lax
from jax.experimental import pallas as pl
from jax.experimental.pallas import tpu as pltpu

F32 = jnp.float32
BF16 = jnp.bfloat16

RMS_EPS = 1e-6
N_MOD = 6
GRID_W = 64
ROPE_THETA = 10000.0
HEADS = 8
Q_LORA = 512
KV_LORA = 256
QK_NOPE = 128
QK_ROPE = 64
V_HEAD = 128
QK_PAD = 256
HG_DIM = 128
HG_CHUNK = 64
HG_GROUP = 8
LANES = 128
KV_TILE = 256
VT_ROWS = 144
LOG2E = 1.4426950408889634
MIB = 1024 * 1024

COL_GATES = 0
COL_HQ = 4096
COL_HI = 5120
COL_HG = 6144
COL_CQ = 7168
COL_CKV = 7680
COL_KR = 7936
N16 = 8192


def _pick(n, candidates):
    for c in candidates:
        if n % c == 0:
            return c
    raise ValueError(f"no tile for {n} in {candidates}")


def _params(sem, vmem_mib):
    return pltpu.CompilerParams(dimension_semantics=sem, vmem_limit_bytes=vmem_mib * MIB)


def _rms(xf):
    return xf * lax.rsqrt(jnp.mean(xf * xf, axis=-1, keepdims=True) + RMS_EPS)


def _silu(x):
    return x * jax.nn.sigmoid(x)


def _mod_kernel(c_ref, w_ref, b_ref, o_ref):
    s = _silu(c_ref[...]).astype(BF16)
    o_ref[...] = jnp.dot(s, w_ref[...].astype(BF16), preferred_element_type=F32) + b_ref[...]


def _mod(cc, w_mod, b_mod):
    d, n = w_mod.shape
    tn = _pick(n, (1024, 512, 256, 128))
    return pl.pallas_call(
        _mod_kernel,
        grid=(n // tn,),
        in_specs=[pl.BlockSpec((8, d), lambda j: (0, 0)),
                  pl.BlockSpec((d, tn), lambda j: (0, j)),
                  pl.BlockSpec((1, tn), lambda j: (0, j))],
        out_specs=pl.BlockSpec((8, tn), lambda j: (0, j)),
        out_shape=jax.ShapeDtypeStruct((8, n), F32),
        compiler_params=_params(("arbitrary",), 40),
        name="mod",
    )(cc, w_mod, b_mod)


def _norm_mod_kernel(x_ref, ctx_ref, mod_ref, g_ref, o_ref, *, d, ctx_row):
    b = pl.program_id(0)
    t = pl.program_id(1)

    def emit(src_ref, row):
        y = _rms(src_ref[...]) * g_ref[0:1, :]
        sh = mod_ref[pl.ds(row, 1), 0:d]
        sc = mod_ref[pl.ds(row, 1), d:2 * d]
        o_ref[...] = (y * (1.0 + sc) + sh).astype(BF16)

    @pl.when(t == 0)
    def _():
        emit(ctx_ref, ctx_row)

    @pl.when(t > 0)
    def _():
        emit(x_ref, b)


def _norm_mod(x, ctx, mod, norm_g):
    bsz, length, d = x.shape
    ctx_len = ctx.shape[1]
    nt = length // ctx_len
    return pl.pallas_call(
        functools.partial(_norm_mod_kernel, d=d, ctx_row=bsz),
        grid=(bsz, nt + 1),
        in_specs=[pl.BlockSpec((None, ctx_len, d), lambda b, t: (b, jnp.maximum(t - 1, 0), 0)),
                  pl.BlockSpec((None, ctx_len, d), lambda b, t: (b, 0, 0)),
                  pl.BlockSpec(mod.shape, lambda b, t: (0, 0)),
                  pl.BlockSpec(norm_g.shape, lambda b, t: (0, 0))],
        out_specs=pl.BlockSpec((None, ctx_len, d), lambda b, t: (b, t, 0)),
        out_shape=jax.ShapeDtypeStruct((bsz, length + ctx_len, d), BF16),
        compiler_params=_params(("arbitrary", "arbitrary"), 32),
        name="norm_mod",
    )(x, ctx, mod, norm_g)


def _matmul_kernel(a_ref, w_ref, o_ref):
    o_ref[...] = jnp.dot(a_ref[...], w_ref[...], preferred_element_type=F32).astype(o_ref.dtype)


def _matmul(a, w, out_dtype, name):
    m, k = a.shape
    n = w.shape[1]
    tm = _pick(m, (1024, 512, 256))
    tn = _pick(n, (1024, 512, 256))
    return pl.pallas_call(
        _matmul_kernel,
        grid=(m // tm, n // tn),
        in_specs=[pl.BlockSpec((tm, k), lambda i, j: (i, 0)),
                  pl.BlockSpec((k, tn), lambda i, j: (0, j))],
        out_specs=pl.BlockSpec((tm, tn), lambda i, j: (i, j)),
        out_shape=jax.ShapeDtypeStruct((m, n), out_dtype),
        compiler_params=_params(("arbitrary", "arbitrary"), 40),
        name=name,
    )(a, w)


def _mla_prep_kernel(cq_ref, ckv_ref, kr_ref, cos_ref, sin_ref, gq_ref, gkv_ref, wqm_ref, wqr_ref, wk_ref, wvt_ref,
                     q_ref, k_ref, vt_ref, *, q_scale):
    cos = cos_ref[...]
    sin = sin_ref[...]
    cqn = (_rms(cq_ref[...].astype(F32)) * gq_ref[...]).astype(BF16)
    main = jnp.dot(cqn, wqm_ref[...], preferred_element_type=F32)
    rot = jnp.dot(cqn, wqr_ref[...], preferred_element_type=F32)
    ckvn = (_rms(ckv_ref[...].astype(F32)) * gkv_ref[...]).astype(BF16)
    kn = jnp.dot(ckvn, wk_ref[...], preferred_element_type=F32)
    vt = lax.dot_general(wvt_ref[...], ckvn, (((1,), (1,)), ((), ())), preferred_element_type=F32)
    tm = vt.shape[1]
    ones_rows = (lax.broadcasted_iota(jnp.int32, (VT_ROWS - V_HEAD, tm), 0) == 0).astype(BF16)
    lane = lax.broadcasted_iota(jnp.int32, cos.shape, 1)
    a = kr_ref[...].astype(F32) * jnp.where(lane < QK_ROPE, cos, sin)
    k_rope = (a + pltpu.roll(a, QK_ROPE, axis=1)).astype(BF16)
    for h in range(HEADS):
        c0 = h * QK_PAD
        q_ref[:, c0:c0 + QK_NOPE] = (main[:, c0:c0 + QK_NOPE] * q_scale).astype(BF16)
        q_rope = main[:, c0 + QK_NOPE:c0 + QK_PAD] * cos + rot[:, h * LANES:(h + 1) * LANES] * sin
        q_ref[:, c0 + QK_NOPE:c0 + QK_PAD] = (q_rope * q_scale).astype(BF16)
        k_ref[:, c0:c0 + QK_NOPE] = kn[:, h * QK_NOPE:(h + 1) * QK_NOPE].astype(BF16)
        k_ref[:, c0 + QK_NOPE:c0 + QK_PAD] = k_rope
        vt_ref[h, 0:V_HEAD, :] = vt[h * V_HEAD:(h + 1) * V_HEAD, :].astype(BF16)
        vt_ref[h, V_HEAD:VT_ROWS, :] = ones_rows


def _mla_prep(p16, cos_t, sin_t, gq, gkv, wq_main, wq_rot, wk, wvt, ctx_len):
    bsz, lk, _ = p16.shape
    tm = ctx_len
    assert tm == KV_TILE
    nt = lk // tm
    whole = lambda arr: pl.BlockSpec(arr.shape, lambda b, t: (0,) * arr.ndim)
    return pl.pallas_call(
        functools.partial(_mla_prep_kernel, q_scale=float((QK_NOPE + QK_ROPE) ** -0.5 * LOG2E)),
        grid=(bsz, nt),
        in_specs=[pl.BlockSpec((None, tm, Q_LORA), lambda b, t: (b, t, COL_CQ // Q_LORA)),
                  pl.BlockSpec((None, tm, KV_LORA), lambda b, t: (b, t, COL_CKV // KV_LORA)),
                  pl.BlockSpec((None, tm, LANES), lambda b, t: (b, t, COL_KR // LANES)),
                  pl.BlockSpec((tm, LANES), lambda b, t: (t, 0)),
                  pl.BlockSpec((tm, LANES), lambda b, t: (t, 0)),
                  whole(gq), whole(gkv), whole(wq_main), whole(wq_rot), whole(wk), whole(wvt)],
        out_specs=[pl.BlockSpec((None, tm, HEADS * QK_PAD), lambda b, t: (b, jnp.maximum(t - 1, 0), 0)),
                   pl.BlockSpec((None, tm, HEADS * QK_PAD), lambda b, t: (b, t, 0)),
                   pl.BlockSpec((None, HEADS, None, VT_ROWS, tm), lambda b, t: (b, 0, t, 0, 0))],
        out_shape=[jax.ShapeDtypeStruct((bsz, lk - ctx_len, HEADS * QK_PAD), BF16),
                   jax.ShapeDtypeStruct((bsz, lk, HEADS * QK_PAD), BF16),
                   jax.ShapeDtypeStruct((bsz, HEADS, nt, VT_ROWS, tm), BF16)],
        compiler_params=_params(("arbitrary", "arbitrary"), 40),
        name="mla_prep",
    )(p16, p16, p16, cos_t, sin_t, gq, gkv, wq_main, wq_rot, wk, wvt)


def _attn_kernel(q_ref, k_ref, vt_ref, o_ref, s_ref, acc_ref, *, kt, nk):
    tq = q_ref.shape[0]
    tk = kt * KV_TILE
    q_tiles = [slice(h * QK_PAD, (h + 1) * QK_PAD) for h in range(tq // QK_PAD)]
    acc_ref[...] = jnp.zeros(acc_ref.shape, F32)

    def scores(j, slot):
        r0 = pl.multiple_of(j * tk, tk)
        ks = k_ref[pl.ds(r0, tk), :]
        cmax = []
        for qs in q_tiles:
            s = lax.dot_general(ks, q_ref[qs, :], (((1,), (1,)), ((), ())), preferred_element_type=F32)
            s_ref[slot, :, qs] = s
            cmax.append(jnp.max(s, axis=0, keepdims=True))
        return jnp.concatenate(cmax, axis=1)

    def accumulate(j, slot, m_old, cmax):
        m_new = jnp.maximum(m_old, cmax)
        alpha = jnp.exp2(m_old - m_new)
        for qs in q_tiles:
            pv = None
            for c in range(kt):
                p = jnp.exp2(s_ref[slot, c * KV_TILE:(c + 1) * KV_TILE, qs] - m_new[:, qs]).astype(BF16)
                d = jnp.dot(vt_ref[j * kt + c], p, preferred_element_type=F32)
                pv = d if pv is None else pv + d
            acc_ref[:, qs] = alpha[:, qs] * acc_ref[:, qs] + pv
        return m_new

    def body(i, carry):
        m, cmax0 = carry
        cmax1 = scores(2 * i + 1, 1)
        m = accumulate(2 * i, 0, m, cmax0)
        cmax0 = scores(2 * i + 2, 0)
        m = accumulate(2 * i + 1, 1, m, cmax1)
        return m, cmax0

    m0 = jnp.full((1, tq), -jnp.inf, F32)
    m, cmax0 = lax.fori_loop(0, (nk - 1) // 2, body, (m0, scores(0, 0)))
    accumulate(nk - 1, 0, m, cmax0)
    acc = acc_ref[...]
    o = acc[0:V_HEAD, :] / acc[V_HEAD:V_HEAD + 1, :]
    o_ref[...] = jnp.transpose(o).astype(o_ref.dtype)


def _attention(q, k, vt):
    bsz, length, _ = q.shape
    lk = k.shape[1]
    nc = vt.shape[2]
    tq = _pick(length, (1024, 512, 256))
    kt = _pick(nc, (3, 1))
    assert (nc // kt) % 2 == 1, "the chunk pipeline handles an odd number of key chunks"
    return pl.pallas_call(
        functools.partial(_attn_kernel, kt=kt, nk=nc // kt),
        grid=(bsz, HEADS, length // tq),
        in_specs=[pl.BlockSpec((None, tq, QK_PAD), lambda b, h, i: (b, i, h)),
                  pl.BlockSpec((None, lk, QK_PAD), lambda b, h, i: (b, 0, h)),
                  pl.BlockSpec((None, None, nc, VT_ROWS, KV_TILE), lambda b, h, i: (b, h, 0, 0, 0))],
        out_specs=pl.BlockSpec((None, tq, V_HEAD), lambda b, h, i: (b, i, h)),
        out_shape=jax.ShapeDtypeStruct((bsz, length, HEADS * V_HEAD), BF16),
        scratch_shapes=[pltpu.VMEM((2, kt * KV_TILE, tq), F32), pltpu.VMEM((VT_ROWS, tq), F32)],
        compiler_params=_params(("arbitrary", "arbitrary", "arbitrary"), 40),
        name="attn",
    )(q, k, vt)


def _hgrn_kernel(hq_ref, hi_ref, hg_ref, ff_ref, fb_ref, lb_ref, gn_ref, trif_ref, trib_ref, o_ref,
                 oacc_ref, st_ref, qd_ref, kd_ref, kt_ref, dec_ref, *, ctx_len, length):
    c = HG_CHUNK
    gsz = HG_GROUP * c
    n_groups = length // gsz
    half = n_groups // 2
    row = lax.broadcasted_iota(jnp.int32, (c, c), 0)
    col = lax.broadcasted_iota(jnp.int32, (c, c), 1)
    tri_c = (row >= col, row <= col)
    lbr = lb_ref[...]
    q_scale = float(HG_DIM ** -0.5)
    nt_dims = (((1,), (1,)), ((), ()))
    tn_dims = (((0,), (0,)), ((), ()))

    def lower_bound(direction):
        a0 = lbr[direction:direction + 1, :]
        a1 = lbr[2 + direction:3 + direction, :]
        mx = jnp.maximum(a0, a1)
        e0 = jnp.exp(a0 - mx)
        e1 = jnp.exp(a1 - mx)
        return e0 / (e0 + e1)

    lbs = (lower_bound(0), lower_bound(1))

    def chunk_order(direction, nchunks):
        return range(nchunks) if direction == 0 else range(nchunks - 1, -1, -1)

    def gates(direction, r0, nchunks):
        n = nchunks * c
        f_ref = ff_ref if direction == 0 else fb_ref
        tri_ref = trif_ref if direction == 0 else trib_ref
        lb = lbs[direction]
        f = lb + (1.0 - lb) * jax.nn.sigmoid(f_ref[pl.ds(r0, n), :])
        kk = 1.0 - f
        lf = jnp.log(f) * LOG2E
        lf_hi = lf.astype(BF16)
        lf_lo = (lf - lf_hi.astype(F32)).astype(BF16)
        bc = jnp.dot(tri_ref[0:n, 0:n], jnp.concatenate([lf_hi, lf_lo], axis=1), preferred_element_type=F32)
        bcum = bc[:, 0:HG_DIM] + bc[:, HG_DIM:2 * HG_DIM]
        last = c - 1 if direction == 0 else 0
        b_last = [bcum[j * c + last:j * c + last + 1, :] for j in range(nchunks)]
        b_last_rows = jnp.concatenate([jnp.broadcast_to(bl, (c, HG_DIM)) for bl in b_last], axis=0)
        k_tail = (kk * jnp.exp2(b_last_rows - bcum)).astype(BF16)
        return kk, bcum, k_tail, jnp.exp2(jnp.concatenate(b_last, axis=0))

    def state_update(st, v, k_tail, decay_row):
        u_t = lax.dot_general(v, k_tail, tn_dims, preferred_element_type=F32)
        return st * decay_row + u_t

    for direction in (0, 1):
        nchunks = ctx_len // c
        _, _, k_tail, decay = gates(direction, 0, nchunks)
        st = jnp.zeros((HG_DIM, HG_DIM), F32)
        for j in chunk_order(direction, nchunks):
            sl = slice(j * c, (j + 1) * c)
            st = state_update(st, hi_ref[sl, :], k_tail[sl], decay[j:j + 1, :])
        st_ref[direction] = st

    def groups_of(s):
        return (s, n_groups - 1 - s)

    pc = 2
    pr = pc * c
    units = HG_GROUP // pc

    def prepare_unit(s, slot, direction, u):
        g = groups_of(s)[direction]
        r0 = pl.multiple_of(ctx_len + g * gsz + u * pr, c)
        rows = slice(u * pr, (u + 1) * pr)
        kk, bcum, k_tail, decay = gates(direction, r0, pc)
        q = _silu(hq_ref[pl.ds(r0, pr), :].astype(F32)) * q_scale
        qd_ref[slot, direction, rows, :] = (q * jnp.exp2(bcum)).astype(BF16)
        kd_ref[slot, direction, rows, :] = (kk * jnp.exp2(-bcum)).astype(BF16)
        kt_ref[slot, direction, rows, :] = k_tail
        dec_ref[slot, direction, u * pc:(u + 1) * pc, :] = decay

    def consume_unit(s, slot, direction, u, st, finalize):
        g = groups_of(s)[direction]
        o0 = pl.multiple_of(g * gsz + u * pr, pr)
        outs = [None] * pc
        for jj in chunk_order(direction, pc):
            j = u * pc + jj
            sl = slice(j * c, (j + 1) * c)
            q_d = qd_ref[slot, direction, sl, :]
            v = hi_ref[pl.ds(ctx_len + o0 + jj * c, c), :]
            sc = lax.dot_general(q_d, kd_ref[slot, direction, sl, :], nt_dims, preferred_element_type=F32)
            sc = jnp.where(tri_c[direction], sc, 0.0).astype(BF16)
            outs[jj] = (jnp.dot(sc, v, preferred_element_type=F32)
                        + lax.dot_general(q_d, st.astype(BF16), nt_dims, preferred_element_type=F32))
            st = state_update(st, v, kt_ref[slot, direction, sl, :], dec_ref[slot, direction, j:j + 1, :])
        out = jnp.concatenate(outs, axis=0)
        if finalize:
            tot = oacc_ref[pl.ds(o0, pr), :] + out
            gate = _silu(hg_ref[pl.ds(ctx_len + o0, pr), :].astype(F32))
            o_ref[pl.ds(o0, pr), :] = (_rms(tot) * gn_ref[...] * gate).astype(o_ref.dtype)
        else:
            oacc_ref[pl.ds(o0, pr), :] = out
        return st

    def run(cons, prep):
        st = [st_ref[0], st_ref[1]] if cons is not None else None
        for idx in range(units):
            for direction in (0, 1):
                u = chunk_order(direction, units)[idx]
                if cons is not None:
                    st[direction] = consume_unit(cons[0], cons[1], direction, u, st[direction], cons[2])
                if prep is not None:
                    prepare_unit(prep[0], prep[1], direction, u)
        if cons is not None:
            st_ref[0] = st[0]
            st_ref[1] = st[1]

    def pair(finalize):
        def body(i, carry):
            run((2 * i, 0, finalize), (2 * i + 1, 1))
            run((2 * i + 1, 1, finalize), (2 * i + 2, 0))
            return carry
        return body

    run(None, (0, 0))
    lax.fori_loop(0, half // 2, pair(False), 0)
    lax.fori_loop(half // 2, n_groups // 2 - 1, pair(True), 0)
    run((n_groups - 2, 0, True), (n_groups - 1, 1))
    run((n_groups - 1, 1, True), None)


def _block_tri(n, c, lower):
    r = jnp.arange(n)
    same = (r[:, None] // c) == (r[None, :] // c)
    tri = (r[:, None] >= r[None, :]) if lower else (r[:, None] <= r[None, :])
    return (same & tri).astype(BF16)


def _hgrn(p16, p32, lb_rows, gn, ctx_len):
    bsz, lk, _ = p16.shape
    length = lk - ctx_len
    gsz = HG_GROUP * HG_CHUNK
    assert (length // gsz) % 4 == 0 and ctx_len <= gsz
    tri_f = _block_tri(gsz, HG_CHUNK, True)
    tri_b = _block_tri(gsz, HG_CHUNK, False)
    strip = lambda col: pl.BlockSpec((None, lk, LANES), lambda b, h: (b, 0, col // LANES + h))
    whole = lambda arr: pl.BlockSpec(arr.shape, lambda b, h: (0,) * arr.ndim)
    return pl.pallas_call(
        functools.partial(_hgrn_kernel, ctx_len=ctx_len, length=length),
        grid=(bsz, HEADS),
        in_specs=[strip(COL_HQ), strip(COL_HI), strip(COL_HG), strip(0), strip(HEADS * HG_DIM),
                  pl.BlockSpec((None, 4, LANES), lambda b, h: (h, 0, 0)),
                  whole(gn), whole(tri_f), whole(tri_b)],
        out_specs=pl.BlockSpec((None, length, LANES), lambda b, h: (b, 0, h)),
        out_shape=jax.ShapeDtypeStruct((bsz, length, HEADS * HG_DIM), BF16),
        scratch_shapes=[pltpu.VMEM((length, HG_DIM), F32), pltpu.VMEM((2, HG_DIM, HG_DIM), F32),
                        pltpu.VMEM((2, 2, gsz, HG_DIM), BF16), pltpu.VMEM((2, 2, gsz, HG_DIM), BF16),
                        pltpu.VMEM((2, 2, gsz, HG_DIM), BF16), pltpu.VMEM((2, 2, HG_GROUP, HG_DIM), F32)],
        compiler_params=_params(("arbitrary", "arbitrary"), 48),
        name="hgrn",
    )(p16, p16, p16, p32, p32, lb_rows, gn, tri_f, tri_b)


def _merge_kernel(attn_ref, hg_ref, gl_ref, x_ref, mod_ref, ng_ref, wa_ref, wh_ref, wo_ref, x1_ref, h2_ref, *, d):
    b = pl.program_id(0)
    ya = jnp.dot(attn_ref[...], wa_ref[...], preferred_element_type=F32)
    yh = jnp.dot(hg_ref[...], wh_ref[...], preferred_element_type=F32)
    g0 = jax.nn.sigmoid(gl_ref[:, 0:d].astype(F32))
    g1 = jax.nn.sigmoid(gl_ref[:, d:2 * d].astype(F32))
    z = (g0 * ya + g1 * yh).astype(BF16)
    o = jnp.dot(z, wo_ref[...], preferred_element_type=F32)
    gt_a = mod_ref[pl.ds(b, 1), 2 * d:3 * d]
    sh_f = mod_ref[pl.ds(b, 1), 3 * d:4 * d]
    sc_f = mod_ref[pl.ds(b, 1), 4 * d:5 * d]
    x1 = x_ref[...] + gt_a * (_rms(o) * ng_ref[1:2, :])
    x1_ref[...] = x1
    h2_ref[...] = (_rms(x1) * ng_ref[2:3, :] * (1.0 + sc_f) + sh_f).astype(BF16)


def _merge(attn, hgn, p16, x, mod, norm_g, w_bm, w_bh, w_o, ctx_len):
    bsz, length, d = x.shape
    tm = ctx_len
    off = ctx_len // tm
    whole = lambda arr: pl.BlockSpec(arr.shape, lambda b, i: (0,) * arr.ndim)
    return pl.pallas_call(
        functools.partial(_merge_kernel, d=d),
        grid=(bsz, length // tm),
        in_specs=[pl.BlockSpec((None, tm, attn.shape[2]), lambda b, i: (b, i, 0)),
                  pl.BlockSpec((None, tm, hgn.shape[2]), lambda b, i: (b, i, 0)),
                  pl.BlockSpec((None, tm, 2 * d), lambda b, i: (b, i + off, COL_GATES // (2 * d))),
                  pl.BlockSpec((None, tm, d), lambda b, i: (b, i, 0)),
                  whole(mod), whole(norm_g), whole(w_bm), whole(w_bh), whole(w_o)],
        out_specs=[pl.BlockSpec((None, tm, d), lambda b, i: (b, i, 0)),
                   pl.BlockSpec((None, tm, d), lambda b, i: (b, i, 0))],
        out_shape=[jax.ShapeDtypeStruct((bsz, length, d), F32),
                   jax.ShapeDtypeStruct((bsz, length, d), BF16)],
        compiler_params=_params(("arbitrary", "arbitrary"), 56),
        name="merge",
    )(attn, hgn, p16, x, mod, norm_g, w_bm, w_bh, w_o)


def _ffn_kernel(h_ref, wa_ref, wb_ref, wo_ref, x1_ref, mod_ref, ng_ref, o_ref, acc_ref, *, d, nf):
    b = pl.program_id(0)
    f = pl.program_id(2)

    @pl.when(f == 0)
    def _():
        acc_ref[...] = jnp.zeros(acc_ref.shape, F32)

    h = h_ref[...]
    a = jnp.dot(h, wa_ref[...], preferred_element_type=F32)
    g = jnp.dot(h, wb_ref[...], preferred_element_type=F32)
    act = (_silu(a) * g).astype(BF16)
    acc_ref[...] += jnp.dot(act, wo_ref[...], preferred_element_type=F32)

    @pl.when(f == nf - 1)
    def _():
        gt_f = mod_ref[pl.ds(b, 1), 5 * d:6 * d]
        o_ref[...] = x1_ref[...] + gt_f * (_rms(acc_ref[...]) * ng_ref[3:4, :])


def _ffn(h2, x1, mod, norm_g, w_fi, w_fo):
    bsz, length, d = x1.shape
    d_ff = w_fo.shape[0]
    tm = _pick(length, (512, 256))
    tf = _pick(d_ff, (512, 256, 128))
    nf = d_ff // tf
    whole = lambda arr: pl.BlockSpec(arr.shape, lambda b, i, f: (0,) * arr.ndim)
    return pl.pallas_call(
        functools.partial(_ffn_kernel, d=d, nf=nf),
        grid=(bsz, length // tm, nf),
        in_specs=[pl.BlockSpec((None, tm, d), lambda b, i, f: (b, i, 0)),
                  pl.BlockSpec((d, tf), lambda b, i, f: (0, f)),
                  pl.BlockSpec((d, tf), lambda b, i, f: (0, nf + f)),
                  pl.BlockSpec((tf, d), lambda b, i, f: (f, 0)),
                  pl.BlockSpec((None, tm, d), lambda b, i, f: (b, i, 0)),
                  whole(mod), whole(norm_g)],
        out_specs=pl.BlockSpec((None, tm, d), lambda b, i, f: (b, i, 0)),
        out_shape=jax.ShapeDtypeStruct((bsz, length, d), F32),
        scratch_shapes=[pltpu.VMEM((tm, d), F32)],
        compiler_params=_params(("arbitrary", "arbitrary", "arbitrary"), 48),
        name="ffn",
    )(h2, w_fi, w_fi, w_fo, x1, mod, norm_g)


def _rotate_half_cols(w):
    half = QK_ROPE // 2
    return jnp.concatenate([-w[..., half:], w[..., :half]], axis=-1)


def _rope_tables(length, ctx_len):
    rows = length // GRID_W
    pairs = QK_ROPE // 4
    row = jnp.repeat(jnp.arange(rows, dtype=F32), GRID_W)
    col = jnp.tile(jnp.arange(GRID_W, dtype=F32), rows)
    inv = ROPE_THETA ** (-jnp.arange(pairs, dtype=F32) / pairs)
    ang = jnp.concatenate([row[:, None] * inv, col[:, None] * inv], axis=-1)
    rep = LANES // (QK_ROPE // 2)
    cos = jnp.concatenate([jnp.ones((ctx_len, LANES), F32), jnp.tile(jnp.cos(ang), (1, rep))], axis=0)
    sin = jnp.concatenate([jnp.zeros((ctx_len, LANES), F32), jnp.tile(jnp.sin(ang), (1, rep))], axis=0)
    return cos, sin


def kernel(x, c, ctx, c_ctx, w_mod, b_mod, norm_g, w_in, mla_q_norm, mla_kv_norm, w_uq, w_ukv,
           hgrn_lb, hgrn_o_norm, w_br_mla, w_br_hgrn, w_out, w_ffn_in, w_ffn_out):
    bsz, length, d = x.shape
    ctx_len = ctx.shape[1]
    depth = w_mod.shape[0]
    assert depth == 1 and hgrn_lb.shape[0] == 2, "kernel is written for the depth-1 block"
    assert bsz < 8 and length % (HG_GROUP * HG_CHUNK) == 0 and ctx_len % HG_CHUNK == 0
    lk = length + ctx_len

    w = w_in[0]
    o = [0]
    for width in (Q_LORA, KV_LORA, QK_ROPE, 1024, 1024, 1024, 1024, 1024, 2 * d):
        o.append(o[-1] + width)
    w_cq, w_ckv, w_kr, w_hq, w_ff, w_fb, w_hi, w_hg, w_gl = [w[:, o[i]:o[i + 1]] for i in range(9)]
    w16 = jnp.concatenate([w_gl, w_hq, w_hi, w_hg, w_cq, w_ckv, w_kr, _rotate_half_cols(w_kr),
                           jnp.zeros((d, N16 - COL_KR - 2 * QK_ROPE), F32)], axis=1).astype(BF16)
    w32 = jnp.concatenate([w_ff, w_fb], axis=1).astype(BF16)

    wq = w_uq[0].reshape(Q_LORA, HEADS, QK_NOPE + QK_ROPE)
    wq_nope, wq_rope = wq[..., :QK_NOPE], wq[..., QK_NOPE:]
    zpad = jnp.zeros((Q_LORA, HEADS, QK_PAD - QK_NOPE - QK_ROPE), F32)
    wq_main = jnp.concatenate([wq_nope, wq_rope, zpad], axis=-1).reshape(Q_LORA, HEADS * QK_PAD).astype(BF16)
    wq_rot = jnp.concatenate([_rotate_half_cols(wq_rope), zpad], axis=-1).reshape(Q_LORA, HEADS * LANES).astype(BF16)
    wkv = w_ukv[0].reshape(KV_LORA, HEADS, QK_NOPE + V_HEAD)
    wk = wkv[..., :QK_NOPE].reshape(KV_LORA, HEADS * QK_NOPE).astype(BF16)
    wvt = wkv[..., QK_NOPE:].reshape(KV_LORA, HEADS * V_HEAD).T.astype(BF16)
    w_bm = w_br_mla[0].astype(BF16)
    w_bh = w_br_hgrn[0].astype(BF16)
    w_o = w_out[0].astype(BF16)
    w_fi = w_ffn_in[0].astype(BF16)
    w_fo = w_ffn_out[0].astype(BF16)
    lb_rows = hgrn_lb.transpose(2, 0, 1, 3).reshape(HEADS, 4, HG_DIM)
    cos_t, sin_t = _rope_tables(length, ctx_len)

    cc = jnp.concatenate([c, c_ctx[None, :], jnp.zeros((8 - bsz - 1, d), F32)], axis=0)
    mod = _mod(cc, w_mod[0], b_mod[0][None, :])
    ng = norm_g[0]

    hcat = _norm_mod(x, ctx, mod, ng)
    h2d = hcat.reshape(bsz * lk, d)
    p16 = _matmul(h2d, w16, BF16, "in_proj16").reshape(bsz, lk, N16)
    p32 = _matmul(h2d, w32, F32, "in_proj32").reshape(bsz, lk, 2 * HEADS * HG_DIM)
    q, k, vt = _mla_prep(p16, cos_t, sin_t, mla_q_norm[0][None, :], mla_kv_norm[0][None, :],
                         wq_main, wq_rot, wk, wvt, ctx_len)
    attn = _attention(q, k, vt)
    hgn = _hgrn(p16, p32, lb_rows, hgrn_o_norm[0][None, :], ctx_len)
    x1, h2 = _merge(attn, hgn, p16, x, mod, ng, w_bm, w_bh, w_o, ctx_len)
    return _ffn(h2, x1, mod, ng, w_fi, w_fo)
```

```python
import functools

import jax
import jax.numpy as jnp
import numpy as np
from jax import lax
from jax.experimental import pallas as pl
from jax.experimental.pallas import tpu as pltpu

F32 = jnp.float32
BF16 = jnp.bfloat16

RMS_EPS = 1e-6
N_MOD = 6
GRID_W = 64
ROPE_THETA = 10000.0
HEADS = 8
Q_LORA = 512
KV_LORA = 256
QK_NOPE = 128
QK_ROPE = 64
V_HEAD = 128
QK_PAD = 256
HG_DIM = 128
HG_CHUNK = 64
HG_GROUP = 8
LANES = 128
KV_TILE = 256
VT_ROWS = 144
LOG2E = 1.4426950408889634
MIB = 1024 * 1024

COL_GATES = 0
COL_HQ = 4096
COL_HI = 5120
COL_HG = 6144
COL_CQ = 7168
COL_CKV = 7680
COL_KR = 7936
N16 = 8192


def _pick(n, candidates):
    for c in candidates:
        if n % c == 0:
            return c
    raise ValueError(f"no tile for {n} in {candidates}")


def _params(sem, vmem_mib):
    return pltpu.CompilerParams(dimension_semantics=sem, vmem_limit_bytes=vmem_mib * MIB)


def _rms(xf):
    return xf * lax.rsqrt(jnp.mean(xf * xf, axis=-1, keepdims=True) + RMS_EPS)


def _silu(x):
    return x * jax.nn.sigmoid(x)


def _mod_kernel(c_ref, w_ref, b_ref, o_ref):
    s = _silu(c_ref[...]).astype(BF16)
    o_ref[...] = jnp.dot(s, w_ref[...].astype(BF16), preferred_element_type=F32) + b_ref[...]


def _mod(cc, w_mod, b_mod):
    d, n = w_mod.shape
    tn = _pick(n, (1024, 512, 256, 128))
    return pl.pallas_call(
        _mod_kernel,
        grid=(n // tn,),
        in_specs=[pl.BlockSpec((8, d), lambda j: (0, 0)),
                  pl.BlockSpec((d, tn), lambda j: (0, j)),
                  pl.BlockSpec((1, tn), lambda j: (0, j))],
        out_specs=pl.BlockSpec((8, tn), lambda j: (0, j)),
        out_shape=jax.ShapeDtypeStruct((8, n), F32),
        compiler_params=_params(("arbitrary",), 40),
        name="mod",
    )(cc, w_mod, b_mod)


def _norm_mod_kernel(x_ref, ctx_ref, mod_ref, g_ref, o_ref, *, d, ctx_row, nt):
    b = pl.program_id(0)
    t = pl.program_id(1)

    def emit(src_ref, row):
        y = _rms(src_ref[...]) * g_ref[0:1, :]
        sh = mod_ref[pl.ds(row, 1), 0:d]
        sc = mod_ref[pl.ds(row, 1), d:2 * d]
        o_ref[...] = (y * (1.0 + sc) + sh).astype(BF16)

    @pl.when(t == nt)
    def _():
        emit(ctx_ref, ctx_row)

    @pl.when(t < nt)
    def _():
        emit(x_ref, b)


def _norm_mod(x, ctx, mod, norm_g):
    bsz, length, d = x.shape
    ctx_len = ctx.shape[1]
    nt = length // ctx_len
    return pl.pallas_call(
        functools.partial(_norm_mod_kernel, d=d, ctx_row=bsz, nt=nt),
        grid=(bsz, nt + 1),
        in_specs=[pl.BlockSpec((None, ctx_len, d), lambda b, t: (b, jnp.minimum(t, nt - 1), 0)),
                  pl.BlockSpec((None, ctx_len, d), lambda b, t: (b, 0, 0)),
                  pl.BlockSpec(mod.shape, lambda b, t: (0, 0)),
                  pl.BlockSpec(norm_g.shape, lambda b, t: (0, 0))],
        out_specs=pl.BlockSpec((None, ctx_len, d), lambda b, t: (b, t, 0)),
        out_shape=jax.ShapeDtypeStruct((bsz, length + ctx_len, d), BF16),
        compiler_params=_params(("arbitrary", "arbitrary"), 32),
        name="norm_mod",
    )(x, ctx, mod, norm_g)


def _matmul_kernel(a_ref, w_ref, o_ref):
    o_ref[...] = jnp.dot(a_ref[...], w_ref[...], preferred_element_type=F32).astype(o_ref.dtype)


def _matmul(a, w, out_dtype, name):
    m, k = a.shape
    n = w.shape[1]
    tm = _pick(m, (1024, 512, 256))
    tn = _pick(n, (1024, 512, 256))
    return pl.pallas_call(
        _matmul_kernel,
        grid=(m // tm, n // tn),
        in_specs=[pl.BlockSpec((tm, k), lambda i, j: (i, 0)),
                  pl.BlockSpec((k, tn), lambda i, j: (0, j))],
        out_specs=pl.BlockSpec((tm, tn), lambda i, j: (i, j)),
        out_shape=jax.ShapeDtypeStruct((m, n), out_dtype),
        compiler_params=_params(("arbitrary", "arbitrary"), 40),
        name=name,
    )(a, w)


def _mla_prep_kernel(cq_ref, ckv_ref, kr_ref, cos_ref, sin_ref, cost_ref, sint_ref, gq_ref, gkv_ref,
                     wqmt_ref, wqrt_ref, wk_ref, wvt_ref, qt_ref, k_ref, vt_ref, *, q_scale, n_lat):
    t = pl.program_id(1)
    nt_dims = (((1,), (1,)), ((), ()))

    @pl.when(t < n_lat)
    def _():
        cqn = (_rms(cq_ref[...].astype(F32)) * gq_ref[...]).astype(BF16)
        main_t = lax.dot_general(wqmt_ref[...], cqn, nt_dims, preferred_element_type=F32)
        rot_t = lax.dot_general(wqrt_ref[...], cqn, nt_dims, preferred_element_type=F32)
        cos_t = cost_ref[...]
        sin_t = sint_ref[...]
        for h in range(HEADS):
            r0 = h * QK_PAD
            qt_ref[h, 0:QK_NOPE, :] = (main_t[r0:r0 + QK_NOPE, :] * q_scale).astype(BF16)
            q_rope = main_t[r0 + QK_NOPE:r0 + QK_PAD, :] * cos_t + rot_t[h * LANES:(h + 1) * LANES, :] * sin_t
            qt_ref[h, QK_NOPE:QK_PAD, :] = (q_rope * q_scale).astype(BF16)

    cos = cos_ref[...]
    sin = sin_ref[...]
    ckvn = (_rms(ckv_ref[...].astype(F32)) * gkv_ref[...]).astype(BF16)
    kn = jnp.dot(ckvn, wk_ref[...], preferred_element_type=F32)
    vt = lax.dot_general(wvt_ref[...], ckvn, nt_dims, preferred_element_type=F32)
    tm = vt.shape[1]
    ones_rows = (lax.broadcasted_iota(jnp.int32, (VT_ROWS - V_HEAD, tm), 0) == 0).astype(BF16)
    lane = lax.broadcasted_iota(jnp.int32, cos.shape, 1)
    a = kr_ref[...].astype(F32) * jnp.where(lane < QK_ROPE, cos, sin)
    k_rope = (a + pltpu.roll(a, QK_ROPE, axis=1)).astype(BF16)
    for h in range(HEADS):
        c0 = h * QK_PAD
        k_ref[:, c0:c0 + QK_NOPE] = kn[:, h * QK_NOPE:(h + 1) * QK_NOPE].astype(BF16)
        k_ref[:, c0 + QK_NOPE:c0 + QK_PAD] = k_rope
        vt_ref[h, 0:V_HEAD, :] = vt[h * V_HEAD:(h + 1) * V_HEAD, :].astype(BF16)
        vt_ref[h, V_HEAD:VT_ROWS, :] = ones_rows


def _mla_prep(p16, tables, gq, gkv, wqm_t, wqr_t, wk, wvt, ctx_len):
    bsz, lk, _ = p16.shape
    tm = KV_TILE
    assert ctx_len == tm
    nt = lk // tm
    n_lat = nt - 1
    cos_t, sin_t, cos_tt, sin_tt = tables
    whole = lambda arr: pl.BlockSpec(arr.shape, lambda b, t: (0,) * arr.ndim)
    return pl.pallas_call(
        functools.partial(_mla_prep_kernel, q_scale=float((QK_NOPE + QK_ROPE) ** -0.5 * LOG2E), n_lat=n_lat),
        grid=(bsz, nt),
        in_specs=[pl.BlockSpec((None, tm, Q_LORA), lambda b, t: (b, t, COL_CQ // Q_LORA)),
                  pl.BlockSpec((None, tm, KV_LORA), lambda b, t: (b, t, COL_CKV // KV_LORA)),
                  pl.BlockSpec((None, tm, LANES), lambda b, t: (b, t, COL_KR // LANES)),
                  pl.BlockSpec((tm, LANES), lambda b, t: (t, 0)),
                  pl.BlockSpec((tm, LANES), lambda b, t: (t, 0)),
                  pl.BlockSpec((LANES, tm), lambda b, t: (0, t)),
                  pl.BlockSpec((LANES, tm), lambda b, t: (0, t)),
                  whole(gq), whole(gkv), whole(wqm_t), whole(wqr_t), whole(wk), whole(wvt)],
        out_specs=[pl.BlockSpec((None, HEADS, None, QK_PAD, tm), lambda b, t: (b, 0, jnp.minimum(t, n_lat - 1), 0, 0)),
                   pl.BlockSpec((None, tm, HEADS * QK_PAD), lambda b, t: (b, t, 0)),
                   pl.BlockSpec((None, HEADS, None, VT_ROWS, tm), lambda b, t: (b, 0, t, 0, 0))],
        out_shape=[jax.ShapeDtypeStruct((bsz, HEADS, n_lat, QK_PAD, tm), BF16),
                   jax.ShapeDtypeStruct((bsz, lk, HEADS * QK_PAD), BF16),
                   jax.ShapeDtypeStruct((bsz, HEADS, nt, VT_ROWS, tm), BF16)],
        compiler_params=_params(("arbitrary", "arbitrary"), 40),
        name="mla_prep",
    )(p16, p16, p16, cos_t, sin_t, cos_tt, sin_tt, gq, gkv, wqm_t, wqr_t, wk, wvt)


def _attn_kernel(qt_ref, k_ref, vt_ref, o_ref, s_ref, acc_ref, *, tq, kt, nk):
    tk = kt * KV_TILE
    sub = tq // KV_TILE
    nq = qt_ref.shape[0] // sub
    q_cols = [slice(h * KV_TILE, (h + 1) * KV_TILE) for h in range(sub)]
    acc_ref[...] = jnp.zeros(acc_ref.shape, F32)

    def scores(i, j, slot):
        r0 = pl.multiple_of(j * tk, tk)
        ks = k_ref[pl.ds(r0, tk), :]
        cmax = []
        for h, qs in enumerate(q_cols):
            s = jnp.dot(ks, qt_ref[i * sub + h], preferred_element_type=F32)
            s_ref[slot, :, qs] = s
            cmax.append(jnp.max(s, axis=0, keepdims=True))
        return jnp.concatenate(cmax, axis=1)

    def accumulate(j, slot, m_old, cmax):
        m_new = jnp.maximum(m_old, cmax)
        alpha = jnp.exp2(m_old - m_new)
        for qs in q_cols:
            pv = None
            for c in range(kt):
                p = jnp.exp2(s_ref[slot, c * KV_TILE:(c + 1) * KV_TILE, qs] - m_new[:, qs]).astype(BF16)
                d = jnp.dot(vt_ref[j * kt + c], p, preferred_element_type=F32)
                pv = d if pv is None else pv + d
            acc_ref[:, qs] = alpha[:, qs] * acc_ref[:, qs] + pv
        return m_new

    neg_inf = jnp.full((1, tq), -jnp.inf, F32)

    def tile(i, slot0, cmax0):
        slot1 = 1 - slot0

        def pair(p, carry):
            m, cm0 = carry
            cm1 = scores(i, 2 * p + 1, slot1)
            m = accumulate(2 * p, slot0, m, cm0)
            cm0 = scores(i, 2 * p + 2, slot0)
            m = accumulate(2 * p + 1, slot1, m, cm1)
            return m, cm0

        m, cm0 = lax.fori_loop(0, (nk - 1) // 2, pair, (neg_inf, cmax0))
        cm_next = scores(jnp.minimum(i + 1, nq - 1), 0, slot1)
        accumulate(nk - 1, slot0, m, cm0)
        acc = acc_ref[...]
        o = acc[0:V_HEAD, :] / acc[V_HEAD:V_HEAD + 1, :]
        o_ref[pl.ds(pl.multiple_of(i * tq, tq), tq), :] = jnp.transpose(o).astype(o_ref.dtype)
        return cm_next

    def two_tiles(t, cmax0):
        return tile(2 * t + 1, 1, tile(2 * t, 0, cmax0))

    lax.fori_loop(0, nq // 2, two_tiles, scores(0, 0, 0))


def _attention(qt, k, vt):
    bsz, heads, n_lat, _, _ = qt.shape
    length = n_lat * KV_TILE
    lk = k.shape[1]
    nc = vt.shape[2]
    tq = _pick(length, (2048, 1024)) // 2
    kt = _pick(nc, (3, 1))
    assert (nc // kt) % 2 == 1 and (length // tq) % 2 == 0
    return pl.pallas_call(
        functools.partial(_attn_kernel, tq=tq, kt=kt, nk=nc // kt),
        grid=(bsz, heads),
        in_specs=[pl.BlockSpec((None, None, n_lat, QK_PAD, KV_TILE), lambda b, h: (b, h, 0, 0, 0)),
                  pl.BlockSpec((None, lk, QK_PAD), lambda b, h: (b, 0, h)),
                  pl.BlockSpec((None, None, nc, VT_ROWS, KV_TILE), lambda b, h: (b, h, 0, 0, 0))],
        out_specs=pl.BlockSpec((None, length, V_HEAD), lambda b, h: (b, 0, h)),
        out_shape=jax.ShapeDtypeStruct((bsz, length, heads * V_HEAD), BF16),
        scratch_shapes=[pltpu.VMEM((2, kt * KV_TILE, tq), F32), pltpu.VMEM((VT_ROWS, tq), F32)],
        compiler_params=_params(("arbitrary", "arbitrary"), 48),
        name="attn",
    )(qt, k, vt)


def _hgrn_kernel(hq_ref, hi_ref, hg_ref, ff_ref, fb_ref, lb_ref, gn_ref, trif_ref, trib_ref, o_ref,
                 oacc_ref, st_ref, qd_ref, kd_ref, kt_ref, dec_ref, *, ctx_len, length):
    c = HG_CHUNK
    gsz = HG_GROUP * c
    n_groups = length // gsz
    half = n_groups // 2
    row = lax.broadcasted_iota(jnp.int32, (c, c), 0)
    col = lax.broadcasted_iota(jnp.int32, (c, c), 1)
    tri_c = (row >= col, row <= col)
    lbr = lb_ref[...]
    q_scale = float(HG_DIM ** -0.5)
    nt_dims = (((1,), (1,)), ((), ()))
    tn_dims = (((0,), (0,)), ((), ()))

    def lower_bound(direction):
        a0 = lbr[direction:direction + 1, :]
        a1 = lbr[2 + direction:3 + direction, :]
        mx = jnp.maximum(a0, a1)
        e0 = jnp.exp(a0 - mx)
        e1 = jnp.exp(a1 - mx)
        return e0 / (e0 + e1)

    lbs = (lower_bound(0), lower_bound(1))

    def chunk_order(direction, nchunks):
        return range(nchunks) if direction == 0 else range(nchunks - 1, -1, -1)

    def gates(direction, r0, nchunks):
        n = nchunks * c
        f_ref = ff_ref if direction == 0 else fb_ref
        tri_ref = trif_ref if direction == 0 else trib_ref
        lb = lbs[direction]
        f = lb + (1.0 - lb) * jax.nn.sigmoid(f_ref[pl.ds(r0, n), :])
        kk = 1.0 - f
        lf = jnp.log(f) * LOG2E
        lf_hi = lf.astype(BF16)
        lf_lo = (lf - lf_hi.astype(F32)).astype(BF16)
        bc = jnp.dot(tri_ref[0:n, 0:n], jnp.concatenate([lf_hi, lf_lo], axis=1), preferred_element_type=F32)
        bcum = bc[:, 0:HG_DIM] + bc[:, HG_DIM:2 * HG_DIM]
        last = c - 1 if direction == 0 else 0
        b_last = [bcum[j * c + last:j * c + last + 1, :] for j in range(nchunks)]
        b_last_rows = jnp.concatenate([jnp.broadcast_to(bl, (c, HG_DIM)) for bl in b_last], axis=0)
        k_tail = (kk * jnp.exp2(b_last_rows - bcum)).astype(BF16)
        return kk, bcum, k_tail, jnp.exp2(jnp.concatenate(b_last, axis=0))

    def state_update(st, v, k_tail, decay_row):
        u_t = lax.dot_general(v, k_tail, tn_dims, preferred_element_type=F32)
        return st * decay_row + u_t

    for direction in (0, 1):
        nchunks = ctx_len // c
        _, _, k_tail, decay = gates(direction, length, nchunks)
        st = jnp.zeros((HG_DIM, HG_DIM), F32)
        for j in chunk_order(direction, nchunks):
            sl = slice(j * c, (j + 1) * c)
            st = state_update(st, hi_ref[length + j * c:length + (j + 1) * c, :], k_tail[sl], decay[j:j + 1, :])
        st_ref[direction] = st

    def groups_of(s):
        return (s, n_groups - 1 - s)

    pc = 2
    pr = pc * c
    units = HG_GROUP // pc

    def prepare_unit(s, slot, direction, u):
        g = groups_of(s)[direction]
        r0 = pl.multiple_of(g * gsz + u * pr, pr)
        rows = slice(u * pr, (u + 1) * pr)
        kk, bcum, k_tail, decay = gates(direction, r0, pc)
        q = _silu(hq_ref[pl.ds(r0, pr), :].astype(F32)) * q_scale
        qd_ref[slot, direction, rows, :] = (q * jnp.exp2(bcum)).astype(BF16)
        kd_ref[slot, direction, rows, :] = (kk * jnp.exp2(-bcum)).astype(BF16)
        kt_ref[slot, direction, rows, :] = k_tail
        dec_ref[slot, direction, u * pc:(u + 1) * pc, :] = decay

    def consume_unit(s, slot, direction, u, st, finalize):
        g = groups_of(s)[direction]
        o0 = pl.multiple_of(g * gsz + u * pr, pr)
        outs = [None] * pc
        for jj in chunk_order(direction, pc):
            j = u * pc + jj
            sl = slice(j * c, (j + 1) * c)
            q_d = qd_ref[slot, direction, sl, :]
            v = hi_ref[pl.ds(o0 + jj * c, c), :]
            sc = lax.dot_general(q_d, kd_ref[slot, direction, sl, :], nt_dims, preferred_element_type=F32)
            sc = jnp.where(tri_c[direction], sc, 0.0).astype(BF16)
            outs[jj] = (jnp.dot(sc, v, preferred_element_type=F32)
                        + lax.dot_general(q_d, st.astype(BF16), nt_dims, preferred_element_type=F32))
            st = state_update(st, v, kt_ref[slot, direction, sl, :], dec_ref[slot, direction, j:j + 1, :])
        out = jnp.concatenate(outs, axis=0)
        if finalize:
            tot = oacc_ref[pl.ds(o0, pr), :] + out
            gate = _silu(hg_ref[pl.ds(o0, pr), :].astype(F32))
            o_ref[pl.ds(o0, pr), :] = (_rms(tot) * gn_ref[...] * gate).astype(o_ref.dtype)
        else:
            oacc_ref[pl.ds(o0, pr), :] = out
        return st

    def run(cons, prep):
        st = [st_ref[0], st_ref[1]] if cons is not None else None
        for idx in range(units):
            for direction in (0, 1):
                u = chunk_order(direction, units)[idx]
                if cons is not None:
                    st[direction] = consume_unit(cons[0], cons[1], direction, u, st[direction], cons[2])
                if prep is not None:
                    prepare_unit(prep[0], prep[1], direction, u)
        if cons is not None:
            st_ref[0] = st[0]
            st_ref[1] = st[1]

    def pair(finalize):
        def body(i, carry):
            run((2 * i, 0, finalize), (2 * i + 1, 1))
            run((2 * i + 1, 1, finalize), (2 * i + 2, 0))
            return carry
        return body

    run(None, (0, 0))
    lax.fori_loop(0, half // 2, pair(False), 0)
    lax.fori_loop(half // 2, n_groups // 2 - 1, pair(True), 0)
    run((n_groups - 2, 0, True), (n_groups - 1, 1))
    run((n_groups - 1, 1, True), None)


def _block_tri(n, c, lower):
    r = jnp.arange(n)
    same = (r[:, None] // c) == (r[None, :] // c)
    tri = (r[:, None] >= r[None, :]) if lower else (r[:, None] <= r[None, :])
    return (same & tri).astype(BF16)


def _hgrn(p16, p32, lb_rows, gn, ctx_len):
    bsz, lk, _ = p16.shape
    length = lk - ctx_len
    gsz = HG_GROUP * HG_CHUNK
    assert (length // gsz) % 4 == 0 and ctx_len <= gsz
    tri_f = _block_tri(gsz, HG_CHUNK, True)
    tri_b = _block_tri(gsz, HG_CHUNK, False)
    strip = lambda col: pl.BlockSpec((None, lk, LANES), lambda b, h: (b, 0, col // LANES + h))
    whole = lambda arr: pl.BlockSpec(arr.shape, lambda b, h: (0,) * arr.ndim)
    return pl.pallas_call(
        functools.partial(_hgrn_kernel, ctx_len=ctx_len, length=length),
        grid=(bsz, HEADS),
        in_specs=[strip(COL_HQ), strip(COL_HI), strip(COL_HG), strip(0), strip(HEADS * HG_DIM),
                  pl.BlockSpec((None, 4, LANES), lambda b, h: (h, 0, 0)),
                  whole(gn), whole(tri_f), whole(tri_b)],
        out_specs=pl.BlockSpec((None, length, LANES), lambda b, h: (b, 0, h)),
        out_shape=jax.ShapeDtypeStruct((bsz, length, HEADS * HG_DIM), BF16),
        scratch_shapes=[pltpu.VMEM((length, HG_DIM), F32), pltpu.VMEM((2, HG_DIM, HG_DIM), F32),
                        pltpu.VMEM((2, 2, gsz, HG_DIM), BF16), pltpu.VMEM((2, 2, gsz, HG_DIM), BF16),
                        pltpu.VMEM((2, 2, gsz, HG_DIM), BF16), pltpu.VMEM((2, 2, HG_GROUP, HG_DIM), F32)],
        compiler_params=_params(("arbitrary", "arbitrary"), 48),
        name="hgrn",
    )(p16, p16, p16, p32, p32, lb_rows, gn, tri_f, tri_b)


def _merge_kernel(attn_ref, hg_ref, gl_ref, x_ref, mod_ref, ng_ref, wa_ref, wh_ref, wo_ref, x1_ref, h2_ref, *, d):
    b = pl.program_id(0)
    ya = jnp.dot(attn_ref[...], wa_ref[...], preferred_element_type=F32)
    yh = jnp.dot(hg_ref[...], wh_ref[...], preferred_element_type=F32)
    g0 = jax.nn.sigmoid(gl_ref[:, 0:d].astype(F32))
    g1 = jax.nn.sigmoid(gl_ref[:, d:2 * d].astype(F32))
    z = (g0 * ya + g1 * yh).astype(BF16)
    o = jnp.dot(z, wo_ref[...], preferred_element_type=F32)
    gt_a = mod_ref[pl.ds(b, 1), 2 * d:3 * d]
    sh_f = mod_ref[pl.ds(b, 1), 3 * d:4 * d]
    sc_f = mod_ref[pl.ds(b, 1), 4 * d:5 * d]
    x1 = x_ref[...] + gt_a * (_rms(o) * ng_ref[1:2, :])
    x1_ref[...] = x1
    h2_ref[...] = (_rms(x1) * ng_ref[2:3, :] * (1.0 + sc_f) + sh_f).astype(BF16)


def _merge(attn, hgn, p16, x, mod, norm_g, w_bm, w_bh, w_o):
    bsz, length, d = x.shape
    tm = _pick(length, (256,))
    whole = lambda arr: pl.BlockSpec(arr.shape, lambda b, i: (0,) * arr.ndim)
    return pl.pallas_call(
        functools.partial(_merge_kernel, d=d),
        grid=(bsz, length // tm),
        in_specs=[pl.BlockSpec((None, tm, attn.shape[2]), lambda b, i: (b, i, 0)),
                  pl.BlockSpec((None, tm, hgn.shape[2]), lambda b, i: (b, i, 0)),
                  pl.BlockSpec((None, tm, 2 * d), lambda b, i: (b, i, COL_GATES // (2 * d))),
                  pl.BlockSpec((None, tm, d), lambda b, i: (b, i, 0)),
                  whole(mod), whole(norm_g), whole(w_bm), whole(w_bh), whole(w_o)],
        out_specs=[pl.BlockSpec((None, tm, d), lambda b, i: (b, i, 0)),
                   pl.BlockSpec((None, tm, d), lambda b, i: (b, i, 0))],
        out_shape=[jax.ShapeDtypeStruct((bsz, length, d), F32),
                   jax.ShapeDtypeStruct((bsz, length, d), BF16)],
        compiler_params=_params(("arbitrary", "arbitrary"), 56),
        name="merge",
    )(attn, hgn, p16, x, mod, norm_g, w_bm, w_bh, w_o)


def _ffn_kernel(h_ref, wa_ref, wb_ref, wo_ref, x1_ref, mod_ref, ng_ref, o_ref, acc_ref, *, d, nf):
    b = pl.program_id(0)
    f = pl.program_id(2)

    @pl.when(f == 0)
    def _():
        acc_ref[...] = jnp.zeros(acc_ref.shape, F32)

    h = h_ref[...]
    a = jnp.dot(h, wa_ref[...], preferred_element_type=F32)
    g = jnp.dot(h, wb_ref[...], preferred_element_type=F32)
    act = (_silu(a) * g).astype(BF16)
    acc_ref[...] += jnp.dot(act, wo_ref[...], preferred_element_type=F32)

    @pl.when(f == nf - 1)
    def _():
        gt_f = mod_ref[pl.ds(b, 1), 5 * d:6 * d]
        o_ref[...] = x1_ref[...] + gt_f * (_rms(acc_ref[...]) * ng_ref[3:4, :])


def _ffn(h2, x1, mod, norm_g, w_fi, w_fo):
    bsz, length, d = x1.shape
    d_ff = w_fo.shape[0]
    tm = _pick(length, (512, 256))
    tf = _pick(d_ff, (512, 256, 128))
    nf = d_ff // tf
    whole = lambda arr: pl.BlockSpec(arr.shape, lambda b, i, f: (0,) * arr.ndim)
    return pl.pallas_call(
        functools.partial(_ffn_kernel, d=d, nf=nf),
        grid=(bsz, length // tm, nf),
        in_specs=[pl.BlockSpec((None, tm, d), lambda b, i, f: (b, i, 0)),
                  pl.BlockSpec((d, tf), lambda b, i, f: (0, f)),
                  pl.BlockSpec((d, tf), lambda b, i, f: (0, nf + f)),
                  pl.BlockSpec((tf, d), lambda b, i, f: (f, 0)),
                  pl.BlockSpec((None, tm, d), lambda b, i, f: (b, i, 0)),
                  whole(mod), whole(norm_g)],
        out_specs=pl.BlockSpec((None, tm, d), lambda b, i, f: (b, i, 0)),
        out_shape=jax.ShapeDtypeStruct((bsz, length, d), F32),
        scratch_shapes=[pltpu.VMEM((tm, d), F32)],
        compiler_params=_params(("arbitrary", "arbitrary", "arbitrary"), 48),
        name="ffn",
    )(h2, w_fi, w_fi, w_fo, x1, mod, norm_g)


def _rotate_half_cols(w):
    half = QK_ROPE // 2
    return jnp.concatenate([-w[..., half:], w[..., :half]], axis=-1)


def _rope_tables(length, ctx_len):
    rows = length // GRID_W
    pairs = QK_ROPE // 4
    row = np.repeat(np.arange(rows, dtype=np.float64), GRID_W)
    col = np.tile(np.arange(GRID_W, dtype=np.float64), rows)
    inv = ROPE_THETA ** (-np.arange(pairs, dtype=np.float64) / pairs)
    ang = np.concatenate([row[:, None] * inv, col[:, None] * inv], axis=-1)
    rep = LANES // (QK_ROPE // 2)
    cos = np.concatenate([np.tile(np.cos(ang), (1, rep)), np.ones((ctx_len, LANES))], axis=0).astype(np.float32)
    sin = np.concatenate([np.tile(np.sin(ang), (1, rep)), np.zeros((ctx_len, LANES))], axis=0).astype(np.float32)
    return tuple(jnp.asarray(t) for t in (cos, sin, np.ascontiguousarray(cos.T), np.ascontiguousarray(sin.T)))


def kernel(x, c, ctx, c_ctx, w_mod, b_mod, norm_g, w_in, mla_q_norm, mla_kv_norm, w_uq, w_ukv,
           hgrn_lb, hgrn_o_norm, w_br_mla, w_br_hgrn, w_out, w_ffn_in, w_ffn_out):
    bsz, length, d = x.shape
    ctx_len = ctx.shape[1]
    depth = w_mod.shape[0]
    assert depth == 1 and hgrn_lb.shape[0] == 2, "kernel is written for the depth-1 block"
    assert bsz < 8 and length % (HG_GROUP * HG_CHUNK) == 0 and ctx_len % HG_CHUNK == 0
    lk = length + ctx_len

    w = w_in[0]
    o = [0]
    for width in (Q_LORA, KV_LORA, QK_ROPE, 1024, 1024, 1024, 1024, 1024, 2 * d):
        o.append(o[-1] + width)
    w_cq, w_ckv, w_kr, w_hq, w_ff, w_fb, w_hi, w_hg, w_gl = [w[:, o[i]:o[i + 1]] for i in range(9)]
    w16 = jnp.concatenate([w_gl, w_hq, w_hi, w_hg, w_cq, w_ckv, w_kr, _rotate_half_cols(w_kr),
                           jnp.zeros((d, N16 - COL_KR - 2 * QK_ROPE), F32)], axis=1).astype(BF16)
    w32 = jnp.concatenate([w_ff, w_fb], axis=1).astype(BF16)

    wq = w_uq[0].reshape(Q_LORA, HEADS, QK_NOPE + QK_ROPE)
    wq_nope, wq_rope = wq[..., :QK_NOPE], wq[..., QK_NOPE:]
    zpad = jnp.zeros((Q_LORA, HEADS, QK_PAD - QK_NOPE - QK_ROPE), F32)
    wqm_t = jnp.concatenate([wq_nope, wq_rope, zpad], axis=-1).reshape(Q_LORA, HEADS * QK_PAD).T.astype(BF16)
    wqr_t = jnp.concatenate([_rotate_half_cols(wq_rope), zpad], axis=-1).reshape(Q_LORA, HEADS * LANES).T.astype(BF16)
    wkv = w_ukv[0].reshape(KV_LORA, HEADS, QK_NOPE + V_HEAD)
    wk = wkv[..., :QK_NOPE].reshape(KV_LORA, HEADS * QK_NOPE).astype(BF16)
    wvt = wkv[..., QK_NOPE:].reshape(KV_LORA, HEADS * V_HEAD).T.astype(BF16)
    w_bm = w_br_mla[0].astype(BF16)
    w_bh = w_br_hgrn[0].astype(BF16)
    w_o = w_out[0].astype(BF16)
    w_fi = w_ffn_in[0].astype(BF16)
    w_fo = w_ffn_out[0].astype(BF16)
    lb_rows = hgrn_lb.transpose(2, 0, 1, 3).reshape(HEADS, 4, HG_DIM)
    tables = _rope_tables(length, ctx_len)

    cc = jnp.concatenate([c, c_ctx[None, :], jnp.zeros((8 - bsz - 1, d), F32)], axis=0)
    mod = _mod(cc, w_mod[0], b_mod[0][None, :])
    ng = norm_g[0]

    hcat = _norm_mod(x, ctx, mod, ng)
    h2d = hcat.reshape(bsz * lk, d)
    p16 = _matmul(h2d, w16, BF16, "in_proj16").reshape(bsz, lk, N16)
    p32 = _matmul(h2d, w32, F32, "in_proj32").reshape(bsz, lk, 2 * HEADS * HG_DIM)
    qt, k, vt = _mla_prep(p16, tables, mla_q_norm[0][None, :], mla_kv_norm[0][None, :],
                          wqm_t, wqr_t, wk, wvt, ctx_len)
    attn = _attention(qt, k, vt)
    hgn = _hgrn(p16, p32, lb_rows, hgrn_o_norm[0][None, :], ctx_len)
    x1, h2 = _merge(attn, hgn, p16, x, mod, ng, w_bm, w_bh, w_o)
    return _ffn(h2, x1, mod, ng, w_fi, w_fo)
```

```python
import functools

import jax
import jax.numpy as jnp
import numpy as np
from jax import lax
from jax.experimental import pallas as pl
from jax.experimental.pallas import tpu as pltpu

F32 = jnp.float32
BF16 = jnp.bfloat16

RMS_EPS = 1e-6
N_MOD = 6
GRID_W = 64
ROPE_THETA = 10000.0
HEADS = 8
Q_LORA = 512
KV_LORA = 256
QK_NOPE = 128
QK_ROPE = 64
V_HEAD = 128
QK_PAD = 256
HG_DIM = 128
HG_CHUNK = 64
HG_GROUP = 8
LANES = 128
NORM_SLAB = 16
KV_TILE = 256
VT_ROWS = 144
LOG2E = 1.4426950408889634
MIB = 1024 * 1024

COL_GATES = 0
COL_HQ = 4096
COL_HI = 5120
COL_HG = 6144
COL_CQ = 7168
COL_CKV = 7680
COL_KR = 7936
N16 = 8192


def _pick(n, candidates):
    for c in candidates:
        if n % c == 0:
            return c
    raise ValueError(f"no tile for {n} in {candidates}")


def _params(sem, vmem_mib):
    return pltpu.CompilerParams(dimension_semantics=sem, vmem_limit_bytes=vmem_mib * MIB)


def _rms(xf):
    return xf * lax.rsqrt(jnp.mean(xf * xf, axis=-1, keepdims=True) + RMS_EPS)


def _silu(x):
    return x * jax.nn.sigmoid(x)


def _mod_kernel(c_ref, w_ref, b_ref, o_ref):
    s = _silu(c_ref[...]).astype(BF16)
    o_ref[...] = jnp.dot(s, w_ref[...].astype(BF16), preferred_element_type=F32) + b_ref[...]


def _mod(cc, w_mod, b_mod):
    d, n = w_mod.shape
    tn = _pick(n, (1024, 512, 256, 128))
    return pl.pallas_call(
        _mod_kernel,
        grid=(n // tn,),
        in_specs=[pl.BlockSpec((8, d), lambda j: (0, 0)),
                  pl.BlockSpec((d, tn), lambda j: (0, j)),
                  pl.BlockSpec((1, tn), lambda j: (0, j))],
        out_specs=pl.BlockSpec((8, tn), lambda j: (0, j)),
        out_shape=jax.ShapeDtypeStruct((8, n), F32),
        compiler_params=_params(("arbitrary",), 40),
        name="mod",
    )(cc, w_mod, b_mod)


def _norm_mod_kernel(x_ref, ctx_ref, mod_ref, g_ref, o_ref, *, d, ctx_row, nt):
    b = pl.program_id(0)
    t = pl.program_id(1)

    def emit(src_ref, row):
        g = g_ref[0:1, :]
        sh = mod_ref[pl.ds(row, 1), 0:d]
        sc1 = 1.0 + mod_ref[pl.ds(row, 1), d:2 * d]

        def slab(r, carry):
            rows = pl.ds(pl.multiple_of(r * NORM_SLAB, NORM_SLAB), NORM_SLAB)
            o_ref[rows, :] = (_rms(src_ref[rows, :]) * g * sc1 + sh).astype(BF16)
            return carry

        lax.fori_loop(0, o_ref.shape[0] // NORM_SLAB, slab, 0, unroll=4)

    @pl.when(t == nt)
    def _():
        emit(ctx_ref, ctx_row)

    @pl.when(t < nt)
    def _():
        emit(x_ref, b)


def _norm_mod(x, ctx, mod, norm_g):
    bsz, length, d = x.shape
    ctx_len = ctx.shape[1]
    nt = length // ctx_len
    return pl.pallas_call(
        functools.partial(_norm_mod_kernel, d=d, ctx_row=bsz, nt=nt),
        grid=(bsz, nt + 1),
        in_specs=[pl.BlockSpec((None, ctx_len, d), lambda b, t: (b, jnp.minimum(t, nt - 1), 0)),
                  pl.BlockSpec((None, ctx_len, d), lambda b, t: (b, 0, 0)),
                  pl.BlockSpec(mod.shape, lambda b, t: (0, 0)),
                  pl.BlockSpec(norm_g.shape, lambda b, t: (0, 0))],
        out_specs=pl.BlockSpec((None, ctx_len, d), lambda b, t: (b, t, 0)),
        out_shape=jax.ShapeDtypeStruct((bsz, length + ctx_len, d), BF16),
        compiler_params=_params(("arbitrary", "arbitrary"), 32),
        name="norm_mod",
    )(x, ctx, mod, norm_g)


def _cast_kernel(w_ref, o_ref):
    o_ref[...] = w_ref[...].astype(o_ref.dtype)


def _cast_bf16(w3d):
    _, k, n = w3d.shape
    tk = _pick(k, (256, 128))
    return pl.pallas_call(
        _cast_kernel,
        grid=(k // tk,),
        in_specs=[pl.BlockSpec((None, tk, n), lambda i: (0, i, 0))],
        out_specs=pl.BlockSpec((tk, n), lambda i: (i, 0)),
        out_shape=jax.ShapeDtypeStruct((k, n), BF16),
        compiler_params=_params(("arbitrary",), 40),
        name="cast_w_in",
    )(w3d)


def _matmul_kernel(a_ref, w_ref, o_ref):
    o_ref[...] = jnp.dot(a_ref[...], w_ref[...], preferred_element_type=F32).astype(o_ref.dtype)


def _matmul(a, w, out_dtype, name):
    m, k = a.shape
    n = w.shape[1]
    tm = _pick(m, (1024, 512, 256))
    tn = _pick(n, (1024, 512, 256))
    return pl.pallas_call(
        _matmul_kernel,
        grid=(m // tm, n // tn),
        in_specs=[pl.BlockSpec((tm, k), lambda i, j: (i, 0)),
                  pl.BlockSpec((k, tn), lambda i, j: (0, j))],
        out_specs=pl.BlockSpec((tm, tn), lambda i, j: (i, j)),
        out_shape=jax.ShapeDtypeStruct((m, n), out_dtype),
        compiler_params=_params(("arbitrary", "arbitrary"), 40),
        name=name,
    )(a, w)


def _mla_prep_kernel(cq_ref, ckv_ref, kr_ref, cos_ref, sin_ref, cost_ref, sint_ref, gq_ref, gkv_ref,
                     wqmt_ref, wqrt_ref, wk_ref, wvt_ref, qt_ref, k_ref, vt_ref, *, q_scale, n_lat):
    t = pl.program_id(1)
    nt_dims = (((1,), (1,)), ((), ()))

    @pl.when(t < n_lat)
    def _():
        cqn = (_rms(cq_ref[...].astype(F32)) * gq_ref[...]).astype(BF16)
        main_t = lax.dot_general(wqmt_ref[...], cqn, nt_dims, preferred_element_type=F32)
        rot_t = lax.dot_general(wqrt_ref[...], cqn, nt_dims, preferred_element_type=F32)
        cos_t = cost_ref[...]
        sin_t = sint_ref[...]
        for h in range(HEADS):
            r0 = h * QK_PAD
            qt_ref[h, 0:QK_NOPE, :] = (main_t[r0:r0 + QK_NOPE, :] * q_scale).astype(BF16)
            q_rope = main_t[r0 + QK_NOPE:r0 + QK_PAD, :] * cos_t + rot_t[h * LANES:(h + 1) * LANES, :] * sin_t
            qt_ref[h, QK_NOPE:QK_PAD, :] = (q_rope * q_scale).astype(BF16)

    cos = cos_ref[...]
    sin = sin_ref[...]
    ckvn = (_rms(ckv_ref[...].astype(F32)) * gkv_ref[...]).astype(BF16)
    kn = jnp.dot(ckvn, wk_ref[...], preferred_element_type=F32)
    vt = lax.dot_general(wvt_ref[...], ckvn, nt_dims, preferred_element_type=F32)
    tm = vt.shape[1]
    ones_rows = (lax.broadcasted_iota(jnp.int32, (VT_ROWS - V_HEAD, tm), 0) == 0).astype(BF16)
    lane = lax.broadcasted_iota(jnp.int32, cos.shape, 1)
    a = kr_ref[...].astype(F32) * jnp.where(lane < QK_ROPE, cos, sin)
    k_rope = (a + pltpu.roll(a, QK_ROPE, axis=1)).astype(BF16)
    for h in range(HEADS):
        c0 = h * QK_PAD
        k_ref[:, c0:c0 + QK_NOPE] = kn[:, h * QK_NOPE:(h + 1) * QK_NOPE].astype(BF16)
        k_ref[:, c0 + QK_NOPE:c0 + QK_PAD] = k_rope
        vt_ref[h, 0:V_HEAD, :] = vt[h * V_HEAD:(h + 1) * V_HEAD, :].astype(BF16)
        vt_ref[h, V_HEAD:VT_ROWS, :] = ones_rows


def _mla_prep(p16, tables, gq, gkv, wqm_t, wqr_t, wk, wvt, ctx_len):
    bsz, lk, _ = p16.shape
    tm = KV_TILE
    assert ctx_len == tm
    nt = lk // tm
    n_lat = nt - 1
    cos_t, sin_t, cos_tt, sin_tt = tables
    whole = lambda arr: pl.BlockSpec(arr.shape, lambda b, t: (0,) * arr.ndim)
    return pl.pallas_call(
        functools.partial(_mla_prep_kernel, q_scale=float((QK_NOPE + QK_ROPE) ** -0.5 * LOG2E), n_lat=n_lat),
        grid=(bsz, nt),
        in_specs=[pl.BlockSpec((None, tm, Q_LORA), lambda b, t: (b, t, COL_CQ // Q_LORA)),
                  pl.BlockSpec((None, tm, KV_LORA), lambda b, t: (b, t, COL_CKV // KV_LORA)),
                  pl.BlockSpec((None, tm, LANES), lambda b, t: (b, t, COL_KR // LANES)),
                  pl.BlockSpec((tm, LANES), lambda b, t: (t, 0)),
                  pl.BlockSpec((tm, LANES), lambda b, t: (t, 0)),
                  pl.BlockSpec((LANES, tm), lambda b, t: (0, t)),
                  pl.BlockSpec((LANES, tm), lambda b, t: (0, t)),
                  whole(gq), whole(gkv), whole(wqm_t), whole(wqr_t), whole(wk), whole(wvt)],
        out_specs=[pl.BlockSpec((None, HEADS, None, QK_PAD, tm), lambda b, t: (b, 0, jnp.minimum(t, n_lat - 1), 0, 0)),
                   pl.BlockSpec((None, tm, HEADS * QK_PAD), lambda b, t: (b, t, 0)),
                   pl.BlockSpec((None, HEADS, None, VT_ROWS, tm), lambda b, t: (b, 0, t, 0, 0))],
        out_shape=[jax.ShapeDtypeStruct((bsz, HEADS, n_lat, QK_PAD, tm), BF16),
                   jax.ShapeDtypeStruct((bsz, lk, HEADS * QK_PAD), BF16),
                   jax.ShapeDtypeStruct((bsz, HEADS, nt, VT_ROWS, tm), BF16)],
        compiler_params=_params(("arbitrary", "arbitrary"), 40),
        name="mla_prep",
    )(p16, p16, p16, cos_t, sin_t, cos_tt, sin_tt, gq, gkv, wqm_t, wqr_t, wk, wvt)


def _attn_kernel(qt_ref, k_ref, vt_ref, o_ref, s_ref, acc_ref, *, tq, kt, nk):
    tk = kt * KV_TILE
    sub = tq // KV_TILE
    nq = qt_ref.shape[0] // sub
    q_cols = [slice(h * KV_TILE, (h + 1) * KV_TILE) for h in range(sub)]
    acc_ref[...] = jnp.zeros(acc_ref.shape, F32)

    def scores(i, j, slot):
        r0 = pl.multiple_of(j * tk, tk)
        ks = k_ref[pl.ds(r0, tk), :]
        cmax = []
        for h, qs in enumerate(q_cols):
            s = jnp.dot(ks, qt_ref[i * sub + h], preferred_element_type=F32)
            s_ref[slot, :, qs] = s
            cmax.append(jnp.max(s, axis=0, keepdims=True))
        return jnp.concatenate(cmax, axis=1)

    def accumulate(j, slot, m_old, cmax):
        m_new = jnp.maximum(m_old, cmax)
        alpha = jnp.exp2(m_old - m_new)
        for qs in q_cols:
            pv = None
            for c in range(kt):
                p = jnp.exp2(s_ref[slot, c * KV_TILE:(c + 1) * KV_TILE, qs] - m_new[:, qs]).astype(BF16)
                d = jnp.dot(vt_ref[j * kt + c], p, preferred_element_type=F32)
                pv = d if pv is None else pv + d
            acc_ref[:, qs] = alpha[:, qs] * acc_ref[:, qs] + pv
        return m_new

    neg_inf = jnp.full((1, tq), -jnp.inf, F32)

    def tile(i, slot0, cmax0):
        slot1 = 1 - slot0

        def pair(p, carry):
            m, cm0 = carry
            cm1 = scores(i, 2 * p + 1, slot1)
            m = accumulate(2 * p, slot0, m, cm0)
            cm0 = scores(i, 2 * p + 2, slot0)
            m = accumulate(2 * p + 1, slot1, m, cm1)
            return m, cm0

        m, cm0 = lax.fori_loop(0, (nk - 1) // 2, pair, (neg_inf, cmax0))
        cm_next = scores(jnp.minimum(i + 1, nq - 1), 0, slot1)
        accumulate(nk - 1, slot0, m, cm0)
        acc = acc_ref[...]
        o = acc[0:V_HEAD, :] / acc[V_HEAD:V_HEAD + 1, :]
        o_ref[pl.ds(pl.multiple_of(i * tq, tq), tq), :] = jnp.transpose(o.astype(o_ref.dtype))
        return cm_next

    def two_tiles(t, cmax0):
        return tile(2 * t + 1, 1, tile(2 * t, 0, cmax0))

    lax.fori_loop(0, nq // 2, two_tiles, scores(0, 0, 0))


def _attention(qt, k, vt):
    bsz, heads, n_lat, _, _ = qt.shape
    length = n_lat * KV_TILE
    lk = k.shape[1]
    nc = vt.shape[2]
    tq = _pick(length, (2048, 1024)) // 2
    kt = _pick(nc, (3, 1))
    assert (nc // kt) % 2 == 1 and (length // tq) % 2 == 0
    return pl.pallas_call(
        functools.partial(_attn_kernel, tq=tq, kt=kt, nk=nc // kt),
        grid=(bsz, heads),
        in_specs=[pl.BlockSpec((None, None, n_lat, QK_PAD, KV_TILE), lambda b, h: (b, h, 0, 0, 0)),
                  pl.BlockSpec((None, lk, QK_PAD), lambda b, h: (b, 0, h)),
                  pl.BlockSpec((None, None, nc, VT_ROWS, KV_TILE), lambda b, h: (b, h, 0, 0, 0))],
        out_specs=pl.BlockSpec((None, length, V_HEAD), lambda b, h: (b, 0, h)),
        out_shape=jax.ShapeDtypeStruct((bsz, length, heads * V_HEAD), BF16),
        scratch_shapes=[pltpu.VMEM((2, kt * KV_TILE, tq), F32), pltpu.VMEM((VT_ROWS, tq), F32)],
        compiler_params=_params(("arbitrary", "arbitrary"), 48),
        name="attn",
    )(qt, k, vt)


def _hgrn_kernel(hq_ref, hi_ref, hg_ref, ff_ref, fb_ref, lb_ref, gn_ref, o_ref,
                 oacc_ref, st_ref, qd_ref, kd_ref, kt_ref, dec_ref, *, ctx_len, length):
    c = HG_CHUNK
    gsz = HG_GROUP * c
    n_groups = length // gsz
    half = n_groups // 2
    row = lax.broadcasted_iota(jnp.int32, (c, c), 0)
    col = lax.broadcasted_iota(jnp.int32, (c, c), 1)
    tri_c = (row >= col, row <= col)
    lbr = lb_ref[...]
    q_scale = float(HG_DIM ** -0.5)
    nt_dims = (((1,), (1,)), ((), ()))
    tn_dims = (((0,), (0,)), ((), ()))

    def lower_bound(direction):
        a0 = lbr[direction:direction + 1, :]
        a1 = lbr[2 + direction:3 + direction, :]
        mx = jnp.maximum(a0, a1)
        e0 = jnp.exp(a0 - mx)
        e1 = jnp.exp(a1 - mx)
        return e0 / (e0 + e1)

    lbs = (lower_bound(0), lower_bound(1))

    def chunk_order(direction, nchunks):
        return range(nchunks) if direction == 0 else range(nchunks - 1, -1, -1)

    def gates(direction, r0, nchunks):
        n = nchunks * c
        f_ref = ff_ref if direction == 0 else fb_ref
        lb = lbs[direction]
        f = lb + (1.0 - lb) * jax.nn.sigmoid(f_ref[pl.ds(r0, n), :])
        kk = 1.0 - f
        lf = jnp.log(f) * LOG2E
        pos = lax.broadcasted_iota(jnp.int32, (n, HG_DIM), 0) & (c - 1)
        bcum = lf
        for sh in (1, 2, 4, 8, 16, 32):
            if direction == 0:
                bcum = bcum + jnp.where(pos >= sh, pltpu.roll(bcum, sh, axis=0), 0.0)
            else:
                bcum = bcum + jnp.where(pos < c - sh, pltpu.roll(bcum, n - sh, axis=0), 0.0)
        last = c - 1 if direction == 0 else 0
        b_last = [bcum[j * c + last:j * c + last + 1, :] for j in range(nchunks)]
        b_last_rows = jnp.concatenate([jnp.broadcast_to(bl, (c, HG_DIM)) for bl in b_last], axis=0)
        k_tail = (kk * jnp.exp2(b_last_rows - bcum)).astype(BF16)
        return kk, bcum, k_tail, jnp.exp2(jnp.concatenate(b_last, axis=0))

    def state_update(st, v, k_tail, decay_row):
        u_t = lax.dot_general(v, k_tail, tn_dims, preferred_element_type=F32)
        return st * decay_row + u_t

    for direction in (0, 1):
        nchunks = ctx_len // c
        _, _, k_tail, decay = gates(direction, length, nchunks)
        st = jnp.zeros((HG_DIM, HG_DIM), F32)
        for j in chunk_order(direction, nchunks):
            sl = slice(j * c, (j + 1) * c)
            st = state_update(st, hi_ref[length + j * c:length + (j + 1) * c, :], k_tail[sl], decay[j:j + 1, :])
        st_ref[direction] = st

    def groups_of(s):
        return (s, n_groups - 1 - s)

    pc = 2
    pr = pc * c
    units = HG_GROUP // pc

    def prepare_unit(s, slot, direction, u):
        g = groups_of(s)[direction]
        r0 = pl.multiple_of(g * gsz + u * pr, pr)
        rows = slice(u * pr, (u + 1) * pr)
        kk, bcum, k_tail, decay = gates(direction, r0, pc)
        q = _silu(hq_ref[pl.ds(r0, pr), :].astype(F32)) * q_scale
        qd_ref[slot, direction, rows, :] = (q * jnp.exp2(bcum)).astype(BF16)
        kd_ref[slot, direction, rows, :] = (kk * jnp.exp2(-bcum)).astype(BF16)
        kt_ref[slot, direction, rows, :] = k_tail
        dec_ref[slot, direction, u * pc:(u + 1) * pc, :] = decay

    def consume_unit(s, slot, direction, u, st, finalize):
        g = groups_of(s)[direction]
        o0 = pl.multiple_of(g * gsz + u * pr, pr)
        outs = [None] * pc
        for jj in chunk_order(direction, pc):
            j = u * pc + jj
            sl = slice(j * c, (j + 1) * c)
            q_d = qd_ref[slot, direction, sl, :]
            v = hi_ref[pl.ds(o0 + jj * c, c), :]
            sc = lax.dot_general(q_d, kd_ref[slot, direction, sl, :], nt_dims, preferred_element_type=F32)
            sc = jnp.where(tri_c[direction], sc, 0.0).astype(BF16)
            outs[jj] = (jnp.dot(sc, v, preferred_element_type=F32)
                        + lax.dot_general(q_d, st.astype(BF16), nt_dims, preferred_element_type=F32))
            st = state_update(st, v, kt_ref[slot, direction, sl, :], dec_ref[slot, direction, j:j + 1, :])
        out = jnp.concatenate(outs, axis=0)
        if finalize:
            tot = oacc_ref[pl.ds(o0, pr), :] + out
            gate = _silu(hg_ref[pl.ds(o0, pr), :].astype(F32))
            o_ref[pl.ds(o0, pr), :] = (_rms(tot) * gn_ref[...] * gate).astype(o_ref.dtype)
        else:
            oacc_ref[pl.ds(o0, pr), :] = out
        return st

    def run(cons, prep):
        st = [st_ref[0], st_ref[1]] if cons is not None else None
        for idx in range(units):
            for direction in (0, 1):
                u = chunk_order(direction, units)[idx]
                if cons is not None:
                    st[direction] = consume_unit(cons[0], cons[1], direction, u, st[direction], cons[2])
                if prep is not None:
                    prepare_unit(prep[0], prep[1], direction, u)
        if cons is not None:
            st_ref[0] = st[0]
            st_ref[1] = st[1]

    def pair(finalize):
        def body(i, carry):
            run((2 * i, 0, finalize), (2 * i + 1, 1))
            run((2 * i + 1, 1, finalize), (2 * i + 2, 0))
            return carry
        return body

    run(None, (0, 0))
    lax.fori_loop(0, half // 2, pair(False), 0)
    lax.fori_loop(half // 2, n_groups // 2 - 1, pair(True), 0)
    run((n_groups - 2, 0, True), (n_groups - 1, 1))
    run((n_groups - 1, 1, True), None)


def _hgrn(p16, p32, lb_rows, gn, ctx_len):
    bsz, lk, _ = p16.shape
    length = lk - ctx_len
    gsz = HG_GROUP * HG_CHUNK
    assert (length // gsz) % 4 == 0 and ctx_len <= gsz
    strip = lambda col: pl.BlockSpec((None, lk, LANES), lambda b, h: (b, 0, col // LANES + h))
    whole = lambda arr: pl.BlockSpec(arr.shape, lambda b, h: (0,) * arr.ndim)
    return pl.pallas_call(
        functools.partial(_hgrn_kernel, ctx_len=ctx_len, length=length),
        grid=(bsz, HEADS),
        in_specs=[strip(COL_HQ), strip(COL_HI), strip(COL_HG), strip(0), strip(HEADS * HG_DIM),
                  pl.BlockSpec((None, 4, LANES), lambda b, h: (h, 0, 0)),
                  whole(gn)],
        out_specs=pl.BlockSpec((None, length, LANES), lambda b, h: (b, 0, h)),
        out_shape=jax.ShapeDtypeStruct((bsz, length, HEADS * HG_DIM), BF16),
        scratch_shapes=[pltpu.VMEM((length, HG_DIM), F32), pltpu.VMEM((2, HG_DIM, HG_DIM), F32),
                        pltpu.VMEM((2, 2, gsz, HG_DIM), BF16), pltpu.VMEM((2, 2, gsz, HG_DIM), BF16),
                        pltpu.VMEM((2, 2, gsz, HG_DIM), BF16), pltpu.VMEM((2, 2, HG_GROUP, HG_DIM), F32)],
        compiler_params=_params(("arbitrary", "arbitrary"), 48),
        name="hgrn",
    )(p16, p16, p16, p32, p32, lb_rows, gn)


def _merge_kernel(attn_ref, hg_ref, gl_ref, x_ref, mod_ref, ng_ref, wa_ref, wh_ref, wo_ref, x1_ref, h2_ref, *, d):
    b = pl.program_id(0)
    ya = jnp.dot(attn_ref[...], wa_ref[...], preferred_element_type=F32)
    yh = jnp.dot(hg_ref[...], wh_ref[...], preferred_element_type=F32)
    g0 = jax.nn.sigmoid(gl_ref[:, 0:d].astype(F32))
    g1 = jax.nn.sigmoid(gl_ref[:, d:2 * d].astype(F32))
    z = (g0 * ya + g1 * yh).astype(BF16)
    o = jnp.dot(z, wo_ref[...], preferred_element_type=F32)
    gt_a = mod_ref[pl.ds(b, 1), 2 * d:3 * d]
    sh_f = mod_ref[pl.ds(b, 1), 3 * d:4 * d]
    sc_f = mod_ref[pl.ds(b, 1), 4 * d:5 * d]
    x1 = x_ref[...] + gt_a * (_rms(o) * ng_ref[1:2, :])
    x1_ref[...] = x1
    h2_ref[...] = (_rms(x1) * ng_ref[2:3, :] * (1.0 + sc_f) + sh_f).astype(BF16)


def _merge(attn, hgn, p16, x, mod, norm_g, w_bm, w_bh, w_o):
    bsz, length, d = x.shape
    tm = _pick(length, (256,))
    whole = lambda arr: pl.BlockSpec(arr.shape, lambda b, i: (0,) * arr.ndim)
    return pl.pallas_call(
        functools.partial(_merge_kernel, d=d),
        grid=(bsz, length // tm),
        in_specs=[pl.BlockSpec((None, tm, attn.shape[2]), lambda b, i: (b, i, 0)),
                  pl.BlockSpec((None, tm, hgn.shape[2]), lambda b, i: (b, i, 0)),
                  pl.BlockSpec((None, tm, 2 * d), lambda b, i: (b, i, COL_GATES // (2 * d))),
                  pl.BlockSpec((None, tm, d), lambda b, i: (b, i, 0)),
                  whole(mod), whole(norm_g), whole(w_bm), whole(w_bh), whole(w_o)],
        out_specs=[pl.BlockSpec((None, tm, d), lambda b, i: (b, i, 0)),
                   pl.BlockSpec((None, tm, d), lambda b, i: (b, i, 0))],
        out_shape=[jax.ShapeDtypeStruct((bsz, length, d), F32),
                   jax.ShapeDtypeStruct((bsz, length, d), BF16)],
        compiler_params=_params(("arbitrary", "arbitrary"), 56),
        name="merge",
    )(attn, hgn, p16, x, mod, norm_g, w_bm, w_bh, w_o)


def _ffn_kernel(h_ref, wa_ref, wb_ref, wo_ref, x1_ref, mod_ref, ng_ref, o_ref, acc_ref, *, d, nf):
    b = pl.program_id(0)
    f = pl.program_id(2)

    @pl.when(f == 0)
    def _():
        acc_ref[...] = jnp.zeros(acc_ref.shape, F32)

    h = h_ref[...]
    a = jnp.dot(h, wa_ref[...], preferred_element_type=F32)
    g = jnp.dot(h, wb_ref[...], preferred_element_type=F32)
    act = (_silu(a) * g).astype(BF16)
    acc_ref[...] += jnp.dot(act, wo_ref[...], preferred_element_type=F32)

    @pl.when(f == nf - 1)
    def _():
        gt_f = mod_ref[pl.ds(b, 1), 5 * d:6 * d]
        o_ref[...] = x1_ref[...] + gt_f * (_rms(acc_ref[...]) * ng_ref[3:4, :])


def _ffn(h2, x1, mod, norm_g, w_fi, w_fo):
    bsz, length, d = x1.shape
    d_ff = w_fo.shape[0]
    tm = _pick(length, (512, 256))
    tf = _pick(d_ff, (512, 256, 128))
    nf = d_ff // tf
    whole = lambda arr: pl.BlockSpec(arr.shape, lambda b, i, f: (0,) * arr.ndim)
    return pl.pallas_call(
        functools.partial(_ffn_kernel, d=d, nf=nf),
        grid=(bsz, length // tm, nf),
        in_specs=[pl.BlockSpec((None, tm, d), lambda b, i, f: (b, i, 0)),
                  pl.BlockSpec((d, tf), lambda b, i, f: (0, f)),
                  pl.BlockSpec((d, tf), lambda b, i, f: (0, nf + f)),
                  pl.BlockSpec((tf, d), lambda b, i, f: (f, 0)),
                  pl.BlockSpec((None, tm, d), lambda b, i, f: (b, i, 0)),
                  whole(mod), whole(norm_g)],
        out_specs=pl.BlockSpec((None, tm, d), lambda b, i, f: (b, i, 0)),
        out_shape=jax.ShapeDtypeStruct((bsz, length, d), F32),
        scratch_shapes=[pltpu.VMEM((tm, d), F32)],
        compiler_params=_params(("arbitrary", "arbitrary", "arbitrary"), 48),
        name="ffn",
    )(h2, w_fi, w_fi, w_fo, x1, mod, norm_g)


def _rotate_half_cols(w):
    half = QK_ROPE // 2
    return jnp.concatenate([-w[..., half:], w[..., :half]], axis=-1)


def _rope_tables(length, ctx_len):
    rows = length // GRID_W
    pairs = QK_ROPE // 4
    row = np.repeat(np.arange(rows, dtype=np.float64), GRID_W)
    col = np.tile(np.arange(GRID_W, dtype=np.float64), rows)
    inv = ROPE_THETA ** (-np.arange(pairs, dtype=np.float64) / pairs)
    ang = np.concatenate([row[:, None] * inv, col[:, None] * inv], axis=-1)
    rep = LANES // (QK_ROPE // 2)
    cos = np.concatenate([np.tile(np.cos(ang), (1, rep)), np.ones((ctx_len, LANES))], axis=0).astype(np.float32)
    sin = np.concatenate([np.tile(np.sin(ang), (1, rep)), np.zeros((ctx_len, LANES))], axis=0).astype(np.float32)
    return tuple(jnp.asarray(t) for t in (cos, sin, np.ascontiguousarray(cos.T), np.ascontiguousarray(sin.T)))


def kernel(x, c, ctx, c_ctx, w_mod, b_mod, norm_g, w_in, mla_q_norm, mla_kv_norm, w_uq, w_ukv,
           hgrn_lb, hgrn_o_norm, w_br_mla, w_br_hgrn, w_out, w_ffn_in, w_ffn_out):
    bsz, length, d = x.shape
    ctx_len = ctx.shape[1]
    depth = w_mod.shape[0]
    assert depth == 1 and hgrn_lb.shape[0] == 2, "kernel is written for the depth-1 block"
    assert bsz < 8 and length % (HG_GROUP * HG_CHUNK) == 0 and ctx_len % HG_CHUNK == 0
    lk = length + ctx_len

    w = _cast_bf16(w_in)
    o = [0]
    for width in (Q_LORA, KV_LORA, QK_ROPE, 1024, 1024, 1024, 1024, 1024, 2 * d):
        o.append(o[-1] + width)
    w_cq, w_ckv, w_kr, w_hq, w_ff, w_fb, w_hi, w_hg, w_gl = [w[:, o[i]:o[i + 1]] for i in range(9)]
    w16 = jnp.concatenate([w_gl, w_hq, w_hi, w_hg, w_cq, w_ckv, w_kr, _rotate_half_cols(w_kr),
                           jnp.zeros((d, N16 - COL_KR - 2 * QK_ROPE), BF16)], axis=1)
    w32 = jnp.concatenate([w_ff, w_fb], axis=1)

    wq = w_uq[0].reshape(Q_LORA, HEADS, QK_NOPE + QK_ROPE)
    wq_nope, wq_rope = wq[..., :QK_NOPE], wq[..., QK_NOPE:]
    zpad = jnp.zeros((Q_LORA, HEADS, QK_PAD - QK_NOPE - QK_ROPE), F32)
    wqm_t = jnp.concatenate([wq_nope, wq_rope, zpad], axis=-1).reshape(Q_LORA, HEADS * QK_PAD).T.astype(BF16)
    wqr_t = jnp.concatenate([_rotate_half_cols(wq_rope), zpad], axis=-1).reshape(Q_LORA, HEADS * LANES).T.astype(BF16)
    wkv = w_ukv[0].reshape(KV_LORA, HEADS, QK_NOPE + V_HEAD)
    wk = wkv[..., :QK_NOPE].reshape(KV_LORA, HEADS * QK_NOPE).astype(BF16)
    wvt = wkv[..., QK_NOPE:].reshape(KV_LORA, HEADS * V_HEAD).T.astype(BF16)
    w_bm = w_br_mla[0].astype(BF16)
    w_bh = w_br_hgrn[0].astype(BF16)
    w_o = w_out[0].astype(BF16)
    w_fi = w_ffn_in[0].astype(BF16)
    w_fo = w_ffn_out[0].astype(BF16)
    lb_rows = hgrn_lb.transpose(2, 0, 1, 3).reshape(HEADS, 4, HG_DIM)
    tables = _rope_tables(length, ctx_len)

    cc = jnp.concatenate([c, c_ctx[None, :], jnp.zeros((8 - bsz - 1, d), F32)], axis=0)
    mod = _mod(cc, w_mod[0], b_mod[0][None, :])
    ng = norm_g[0]

    hcat = _norm_mod(x, ctx, mod, ng)
    h2d = hcat.reshape(bsz * lk, d)
    p16 = _matmul(h2d, w16, BF16, "in_proj16").reshape(bsz, lk, N16)
    p32 = _matmul(h2d, w32, F32, "in_proj32").reshape(bsz, lk, 2 * HEADS * HG_DIM)
    qt, k, vt = _mla_prep(p16, tables, mla_q_norm[0][None, :], mla_kv_norm[0][None, :],
                          wqm_t, wqr_t, wk, wvt, ctx_len)
    attn = _attention(qt, k, vt)
    hgn = _hgrn(p16, p32, lb_rows, hgrn_o_norm[0][None, :], ctx_len)
    x1, h2 = _merge(attn, hgn, p16, x, mod, ng, w_bm, w_bh, w_o)
    return _ffn(h2, x1, mod, ng, w_fi, w_fo)
```

```python
import functools

import jax
import jax.numpy as jnp
import numpy as np
from jax import lax
from jax.experimental import pallas as pl
from jax.experimental.pallas import tpu as pltpu

F32 = jnp.float32
BF16 = jnp.bfloat16

RMS_EPS = 1e-6
N_MOD = 6
GRID_W = 64
ROPE_THETA = 10000.0
HEADS = 8
Q_LORA = 512
KV_LORA = 256
QK_NOPE = 128
QK_ROPE = 64
V_HEAD = 128
QK_PAD = 256
HG_DIM = 128
HG_CHUNK = 64
HG_GROUP = 8
LANES = 128
NORM_SLAB = 16
KV_TILE = 256
VT_ROWS = 144
LOG2E = 1.4426950408889634
MIB = 1024 * 1024

COL_GATES = 0
COL_HQ = 4096
COL_HI = 5120
COL_HG = 6144
COL_CQ = 7168
COL_CKV = 7680
COL_KR = 7936
N16 = 8192


def _pick(n, candidates):
    for c in candidates:
        if n % c == 0:
            return c
    raise ValueError(f"no tile for {n} in {candidates}")


def _params(sem, vmem_mib):
    return pltpu.CompilerParams(dimension_semantics=sem, vmem_limit_bytes=vmem_mib * MIB)


def _rms(xf):
    return xf * lax.rsqrt(jnp.mean(xf * xf, axis=-1, keepdims=True) + RMS_EPS)


def _silu(x):
    return x * jax.nn.sigmoid(x)


def _mod_kernel(c_ref, w_ref, b_ref, o_ref):
    s = _silu(c_ref[...]).astype(BF16)
    o_ref[...] = jnp.dot(s, w_ref[...].astype(BF16), preferred_element_type=F32) + b_ref[...]


def _mod(cc, w_mod, b_mod):
    d, n = w_mod.shape
    tn = _pick(n, (1024, 512, 256, 128))
    return pl.pallas_call(
        _mod_kernel,
        grid=(n // tn,),
        in_specs=[pl.BlockSpec((8, d), lambda j: (0, 0)),
                  pl.BlockSpec((d, tn), lambda j: (0, j)),
                  pl.BlockSpec((1, tn), lambda j: (0, j))],
        out_specs=pl.BlockSpec((8, tn), lambda j: (0, j)),
        out_shape=jax.ShapeDtypeStruct((8, n), F32),
        compiler_params=_params(("arbitrary",), 40),
        name="mod",
    )(cc, w_mod, b_mod)


def _norm_mod_kernel(x_ref, ctx_ref, mod_ref, g_ref, o_ref, *, d, ctx_row, nt):
    b = pl.program_id(0)
    t = pl.program_id(1)

    def emit(src_ref, row):
        g = g_ref[0:1, :]
        sh = mod_ref[pl.ds(row, 1), 0:d]
        sc1 = 1.0 + mod_ref[pl.ds(row, 1), d:2 * d]

        def slab(r, carry):
            rows = pl.ds(pl.multiple_of(r * NORM_SLAB, NORM_SLAB), NORM_SLAB)
            o_ref[rows, :] = (_rms(src_ref[rows, :]) * g * sc1 + sh).astype(BF16)
            return carry

        lax.fori_loop(0, o_ref.shape[0] // NORM_SLAB, slab, 0, unroll=4)

    @pl.when(t == nt)
    def _():
        emit(ctx_ref, ctx_row)

    @pl.when(t < nt)
    def _():
        emit(x_ref, b)


def _norm_mod(x, ctx, mod, norm_g):
    bsz, length, d = x.shape
    ctx_len = ctx.shape[1]
    nt = length // ctx_len
    return pl.pallas_call(
        functools.partial(_norm_mod_kernel, d=d, ctx_row=bsz, nt=nt),
        grid=(bsz, nt + 1),
        in_specs=[pl.BlockSpec((None, ctx_len, d), lambda b, t: (b, jnp.minimum(t, nt - 1), 0)),
                  pl.BlockSpec((None, ctx_len, d), lambda b, t: (b, 0, 0)),
                  pl.BlockSpec(mod.shape, lambda b, t: (0, 0)),
                  pl.BlockSpec(norm_g.shape, lambda b, t: (0, 0))],
        out_specs=pl.BlockSpec((None, ctx_len, d), lambda b, t: (b, t, 0)),
        out_shape=jax.ShapeDtypeStruct((bsz, length + ctx_len, d), BF16),
        compiler_params=_params(("arbitrary", "arbitrary"), 32),
        name="norm_mod",
    )(x, ctx, mod, norm_g)


def _w_in_kernel(w_ref, o16_ref, o32_ref, *, d):
    o = [0]
    for width in (Q_LORA, KV_LORA, QK_ROPE, 1024, 1024, 1024, 1024, 1024, 2 * d):
        o.append(o[-1] + width)
    grab = lambda lo, hi: w_ref[:, lo:hi].astype(BF16)
    o16_ref[:, COL_GATES:COL_GATES + 2 * d] = grab(o[8], o[9])
    o16_ref[:, COL_HQ:COL_HQ + 1024] = grab(o[3], o[4])
    o16_ref[:, COL_HI:COL_HI + 1024] = grab(o[6], o[7])
    o16_ref[:, COL_HG:COL_HG + 1024] = grab(o[7], o[8])
    o16_ref[:, COL_CQ:COL_KR] = grab(o[0], o[2])
    kr = grab(o[2], o[3])
    half = QK_ROPE // 2
    tail = jnp.zeros((kr.shape[0], N16 - COL_KR - 2 * QK_ROPE), BF16)
    o16_ref[:, COL_KR:N16] = jnp.concatenate([kr, -kr[:, half:], kr[:, :half], tail], axis=1)
    o32_ref[:, 0:1024] = grab(o[4], o[5])
    o32_ref[:, 1024:2048] = grab(o[5], o[6])


def _w_in_prep(w3d):
    _, k, n = w3d.shape
    tk = _pick(k, (256, 128))
    return pl.pallas_call(
        functools.partial(_w_in_kernel, d=k),
        grid=(k // tk,),
        in_specs=[pl.BlockSpec((None, tk, n), lambda i: (0, i, 0))],
        out_specs=[pl.BlockSpec((tk, N16), lambda i: (i, 0)),
                   pl.BlockSpec((tk, 2 * HEADS * HG_DIM), lambda i: (i, 0))],
        out_shape=[jax.ShapeDtypeStruct((k, N16), BF16), jax.ShapeDtypeStruct((k, 2 * HEADS * HG_DIM), BF16)],
        compiler_params=_params(("arbitrary",), 48),
        name="w_in_prep",
    )(w3d)


def _matmul_kernel(a_ref, w_ref, o_ref):
    o_ref[...] = jnp.dot(a_ref[...], w_ref[...], preferred_element_type=F32).astype(o_ref.dtype)


def _matmul(a, w, out_dtype, name):
    m, k = a.shape
    n = w.shape[1]
    tm = _pick(m, (1024, 512, 256))
    tn = _pick(n, (1024, 512, 256))
    return pl.pallas_call(
        _matmul_kernel,
        grid=(m // tm, n // tn),
        in_specs=[pl.BlockSpec((tm, k), lambda i, j: (i, 0)),
                  pl.BlockSpec((k, tn), lambda i, j: (0, j))],
        out_specs=pl.BlockSpec((tm, tn), lambda i, j: (i, j)),
        out_shape=jax.ShapeDtypeStruct((m, n), out_dtype),
        compiler_params=_params(("arbitrary", "arbitrary"), 40),
        name=name,
    )(a, w)


def _mla_prep_kernel(cq_ref, ckv_ref, kr_ref, cos_ref, sin_ref, cost_ref, sint_ref, gq_ref, gkv_ref,
                     wqmt_ref, wk_ref, wvt_ref, qt_ref, k_ref, vt_ref, *, q_scale, n_lat):
    t = pl.program_id(1)
    nt_dims = (((1,), (1,)), ((), ()))
    half = QK_ROPE // 2

    @pl.when(t < n_lat)
    def _():
        cqn = (_rms(cq_ref[...].astype(F32)) * gq_ref[...]).astype(BF16)
        main_t = lax.dot_general(wqmt_ref[...], cqn, nt_dims, preferred_element_type=F32)
        cos_t = cost_ref[...]
        sin_t = sint_ref[...]
        for h in range(HEADS):
            r0 = h * QK_PAD
            qt_ref[h, 0:QK_NOPE, :] = (main_t[r0:r0 + QK_NOPE, :] * q_scale).astype(BF16)
            rope = main_t[r0 + QK_NOPE:r0 + QK_PAD, :]
            swapped = jnp.concatenate([rope[half:2 * half], rope[0:half], rope[2 * half:]], axis=0)
            qt_ref[h, QK_NOPE:QK_PAD, :] = ((rope * cos_t + swapped * sin_t) * q_scale).astype(BF16)

    cos = cos_ref[...]
    sin = sin_ref[...]
    ckvn = (_rms(ckv_ref[...].astype(F32)) * gkv_ref[...]).astype(BF16)
    kn = jnp.dot(ckvn, wk_ref[...], preferred_element_type=F32)
    vt = lax.dot_general(wvt_ref[...], ckvn, nt_dims, preferred_element_type=F32)
    tm = vt.shape[1]
    ones_rows = (lax.broadcasted_iota(jnp.int32, (VT_ROWS - V_HEAD, tm), 0) == 0).astype(BF16)
    lane = lax.broadcasted_iota(jnp.int32, cos.shape, 1)
    a = kr_ref[...].astype(F32) * jnp.where(lane < QK_ROPE, cos, sin)
    k_rope = (a + pltpu.roll(a, QK_ROPE, axis=1)).astype(BF16)
    for h in range(HEADS):
        c0 = h * QK_PAD
        k_ref[:, c0:c0 + QK_NOPE] = kn[:, h * QK_NOPE:(h + 1) * QK_NOPE].astype(BF16)
        k_ref[:, c0 + QK_NOPE:c0 + QK_PAD] = k_rope
        vt_ref[h, 0:V_HEAD, :] = vt[h * V_HEAD:(h + 1) * V_HEAD, :].astype(BF16)
        vt_ref[h, V_HEAD:VT_ROWS, :] = ones_rows


def _mla_prep(p16, tables, gq, gkv, wqm_t, wk, wvt, ctx_len):
    bsz, lk, _ = p16.shape
    tm = KV_TILE
    assert ctx_len == tm
    nt = lk // tm
    n_lat = nt - 1
    cos_t, sin_t, cos_tt, sin_tt = tables
    whole = lambda arr: pl.BlockSpec(arr.shape, lambda b, t: (0,) * arr.ndim)
    return pl.pallas_call(
        functools.partial(_mla_prep_kernel, q_scale=float((QK_NOPE + QK_ROPE) ** -0.5 * LOG2E), n_lat=n_lat),
        grid=(bsz, nt),
        in_specs=[pl.BlockSpec((None, tm, Q_LORA), lambda b, t: (b, t, COL_CQ // Q_LORA)),
                  pl.BlockSpec((None, tm, KV_LORA), lambda b, t: (b, t, COL_CKV // KV_LORA)),
                  pl.BlockSpec((None, tm, LANES), lambda b, t: (b, t, COL_KR // LANES)),
                  pl.BlockSpec((tm, LANES), lambda b, t: (t, 0)),
                  pl.BlockSpec((tm, LANES), lambda b, t: (t, 0)),
                  pl.BlockSpec((LANES, tm), lambda b, t: (0, t)),
                  pl.BlockSpec((LANES, tm), lambda b, t: (0, t)),
                  whole(gq), whole(gkv), whole(wqm_t), whole(wk), whole(wvt)],
        out_specs=[pl.BlockSpec((None, HEADS, None, QK_PAD, tm), lambda b, t: (b, 0, jnp.minimum(t, n_lat - 1), 0, 0)),
                   pl.BlockSpec((None, tm, HEADS * QK_PAD), lambda b, t: (b, t, 0)),
                   pl.BlockSpec((None, HEADS, None, VT_ROWS, tm), lambda b, t: (b, 0, t, 0, 0))],
        out_shape=[jax.ShapeDtypeStruct((bsz, HEADS, n_lat, QK_PAD, tm), BF16),
                   jax.ShapeDtypeStruct((bsz, lk, HEADS * QK_PAD), BF16),
                   jax.ShapeDtypeStruct((bsz, HEADS, nt, VT_ROWS, tm), BF16)],
        compiler_params=_params(("arbitrary", "arbitrary"), 40),
        name="mla_prep",
    )(p16, p16, p16, cos_t, sin_t, cos_tt, sin_tt, gq, gkv, wqm_t, wk, wvt)


def _attn_kernel(qt_ref, k_ref, vt_ref, o_ref, s_ref, acc_ref, *, tq, kt, nk):
    tk = kt * KV_TILE
    sub = tq // KV_TILE
    nq = qt_ref.shape[0] // sub
    q_cols = [slice(h * KV_TILE, (h + 1) * KV_TILE) for h in range(sub)]
    acc_ref[...] = jnp.zeros(acc_ref.shape, F32)

    def scores(i, j, slot):
        r0 = pl.multiple_of(j * tk, tk)
        ks = k_ref[pl.ds(r0, tk), :]
        cmax = []
        for h, qs in enumerate(q_cols):
            s = jnp.dot(ks, qt_ref[i * sub + h], preferred_element_type=F32)
            s_ref[slot, :, qs] = s
            cmax.append(jnp.max(s, axis=0, keepdims=True))
        return jnp.concatenate(cmax, axis=1)

    def accumulate(j, slot, m_old, cmax):
        m_new = jnp.maximum(m_old, cmax)
        alpha = jnp.exp2(m_old - m_new)
        for qs in q_cols:
            pv = None
            for c in range(kt):
                p = jnp.exp2(s_ref[slot, c * KV_TILE:(c + 1) * KV_TILE, qs] - m_new[:, qs]).astype(BF16)
                d = jnp.dot(vt_ref[j * kt + c], p, preferred_element_type=F32)
                pv = d if pv is None else pv + d
            acc_ref[:, qs] = alpha[:, qs] * acc_ref[:, qs] + pv
        return m_new

    neg_inf = jnp.full((1, tq), -jnp.inf, F32)

    def tile(i, cm0):
        def triple(t, carry):
            m, cm0 = carry
            cm1 = scores(i, 3 * t + 1, 1)
            m = accumulate(3 * t, 0, m, cm0)
            cm2 = scores(i, 3 * t + 2, 2)
            m = accumulate(3 * t + 1, 1, m, cm1)
            cm0 = scores(i, 3 * t + 3, 0)
            m = accumulate(3 * t + 2, 2, m, cm2)
            return m, cm0

        m, cm0 = lax.fori_loop(0, (nk - 2) // 3, triple, (neg_inf, cm0))
        cm1 = scores(i, nk - 1, 1)
        m = accumulate(nk - 2, 0, m, cm0)
        cm_next = scores(jnp.minimum(i + 1, nq - 1), 0, 0)
        accumulate(nk - 1, 1, m, cm1)
        acc = acc_ref[...]
        o = acc[0:V_HEAD, :] / acc[V_HEAD:V_HEAD + 1, :]
        o_ref[pl.ds(pl.multiple_of(i * tq, tq), tq), :] = jnp.transpose(o.astype(o_ref.dtype))
        return cm_next

    lax.fori_loop(0, nq, tile, scores(0, 0, 0))


def _attention(qt, k, vt):
    bsz, heads, n_lat, _, _ = qt.shape
    length = n_lat * KV_TILE
    lk = k.shape[1]
    nc = vt.shape[2]
    tq = _pick(length, (1024,))
    kt = _pick(nc, (3, 1))
    assert (nc // kt) % 3 == 2, "the three-slot chunk pipeline peels two trailing chunks per query tile"
    return pl.pallas_call(
        functools.partial(_attn_kernel, tq=tq, kt=kt, nk=nc // kt),
        grid=(bsz, heads),
        in_specs=[pl.BlockSpec((None, None, n_lat, QK_PAD, KV_TILE), lambda b, h: (b, h, 0, 0, 0)),
                  pl.BlockSpec((None, lk, QK_PAD), lambda b, h: (b, 0, h)),
                  pl.BlockSpec((None, None, nc, VT_ROWS, KV_TILE), lambda b, h: (b, h, 0, 0, 0))],
        out_specs=pl.BlockSpec((None, length, V_HEAD), lambda b, h: (b, 0, h)),
        out_shape=jax.ShapeDtypeStruct((bsz, length, heads * V_HEAD), BF16),
        scratch_shapes=[pltpu.VMEM((3, kt * KV_TILE, tq), F32), pltpu.VMEM((VT_ROWS, tq), F32)],
        compiler_params=_params(("arbitrary", "arbitrary"), 48),
        name="attn",
    )(qt, k, vt)


def _hgrn_kernel(hq_ref, hi_ref, hg_ref, ff_ref, fb_ref, lb_ref, gn_ref, o_ref,
                 oacc_ref, st_ref, qd_ref, kd_ref, kt_ref, dec_ref, *, ctx_len, length):
    c = HG_CHUNK
    gsz = HG_GROUP * c
    n_groups = length // gsz
    half = n_groups // 2
    row = lax.broadcasted_iota(jnp.int32, (c, c), 0)
    col = lax.broadcasted_iota(jnp.int32, (c, c), 1)
    tri_c = (row >= col, row <= col)
    lbr = lb_ref[...]
    q_scale = float(HG_DIM ** -0.5)
    nt_dims = (((1,), (1,)), ((), ()))
    tn_dims = (((0,), (0,)), ((), ()))

    def lower_bound(direction):
        a0 = lbr[direction:direction + 1, :]
        a1 = lbr[2 + direction:3 + direction, :]
        mx = jnp.maximum(a0, a1)
        e0 = jnp.exp(a0 - mx)
        e1 = jnp.exp(a1 - mx)
        return e0 / (e0 + e1)

    lbs = (lower_bound(0), lower_bound(1))

    def chunk_order(direction, nchunks):
        return range(nchunks) if direction == 0 else range(nchunks - 1, -1, -1)

    def gates(direction, r0, nchunks):
        n = nchunks * c
        f_ref = ff_ref if direction == 0 else fb_ref
        lb = lbs[direction]
        f = lb + (1.0 - lb) * jax.nn.sigmoid(f_ref[pl.ds(r0, n), :])
        kk = 1.0 - f
        lf = jnp.log(f) * LOG2E
        pos = lax.broadcasted_iota(jnp.int32, (n, HG_DIM), 0) & (c - 1)
        bcum = lf
        for sh in (1, 2, 4, 8, 16, 32):
            if direction == 0:
                bcum = bcum + jnp.where(pos >= sh, pltpu.roll(bcum, sh, axis=0), 0.0)
            else:
                bcum = bcum + jnp.where(pos < c - sh, pltpu.roll(bcum, n - sh, axis=0), 0.0)
        last = c - 1 if direction == 0 else 0
        b_last = [bcum[j * c + last:j * c + last + 1, :] for j in range(nchunks)]
        b_last_rows = jnp.concatenate([jnp.broadcast_to(bl, (c, HG_DIM)) for bl in b_last], axis=0)
        k_tail = (kk * jnp.exp2(b_last_rows - bcum)).astype(BF16)
        return kk, bcum, k_tail, jnp.exp2(jnp.concatenate(b_last, axis=0))

    def state_update(st, v, k_tail, decay_row):
        u_t = lax.dot_general(v, k_tail, tn_dims, preferred_element_type=F32)
        return st * decay_row + u_t

    for direction in (0, 1):
        nchunks = ctx_len // c
        _, _, k_tail, decay = gates(direction, length, nchunks)
        st = jnp.zeros((HG_DIM, HG_DIM), F32)
        for j in chunk_order(direction, nchunks):
            sl = slice(j * c, (j + 1) * c)
            st = state_update(st, hi_ref[length + j * c:length + (j + 1) * c, :], k_tail[sl], decay[j:j + 1, :])
        st_ref[direction] = st

    def groups_of(s):
        return (s, n_groups - 1 - s)

    pc = 2
    pr = pc * c
    units = HG_GROUP // pc

    def prepare_unit(s, slot, direction, u):
        g = groups_of(s)[direction]
        r0 = pl.multiple_of(g * gsz + u * pr, pr)
        rows = slice(u * pr, (u + 1) * pr)
        kk, bcum, k_tail, decay = gates(direction, r0, pc)
        q = _silu(hq_ref[pl.ds(r0, pr), :].astype(F32)) * q_scale
        qd_ref[slot, direction, rows, :] = (q * jnp.exp2(bcum)).astype(BF16)
        kd_ref[slot, direction, rows, :] = (kk * jnp.exp2(-bcum)).astype(BF16)
        kt_ref[slot, direction, rows, :] = k_tail
        dec_ref[slot, direction, u * pc:(u + 1) * pc, :] = decay

    def consume_unit(s, slot, direction, u, st, finalize):
        g = groups_of(s)[direction]
        o0 = pl.multiple_of(g * gsz + u * pr, pr)
        outs = [None] * pc
        for jj in chunk_order(direction, pc):
            j = u * pc + jj
            sl = slice(j * c, (j + 1) * c)
            q_d = qd_ref[slot, direction, sl, :]
            v = hi_ref[pl.ds(o0 + jj * c, c), :]
            sc = lax.dot_general(q_d, kd_ref[slot, direction, sl, :], nt_dims, preferred_element_type=F32)
            sc = jnp.where(tri_c[direction], sc, 0.0).astype(BF16)
            outs[jj] = (jnp.dot(sc, v, preferred_element_type=F32)
                        + lax.dot_general(q_d, st.astype(BF16), nt_dims, preferred_element_type=F32))
            st = state_update(st, v, kt_ref[slot, direction, sl, :], dec_ref[slot, direction, j:j + 1, :])
        out = jnp.concatenate(outs, axis=0)
        if finalize:
            tot = oacc_ref[pl.ds(o0, pr), :] + out
            gate = _silu(hg_ref[pl.ds(o0, pr), :].astype(F32))
            o_ref[pl.ds(o0, pr), :] = (_rms(tot) * gn_ref[...] * gate).astype(o_ref.dtype)
        else:
            oacc_ref[pl.ds(o0, pr), :] = out
        return st

    def run(cons, prep):
        st = [st_ref[0], st_ref[1]] if cons is not None else None
        for idx in range(units):
            for direction in (0, 1):
                u = chunk_order(direction, units)[idx]
                if cons is not None:
                    st[direction] = consume_unit(cons[0], cons[1], direction, u, st[direction], cons[2])
                if prep is not None:
                    prepare_unit(prep[0], prep[1], direction, u)
        if cons is not None:
            st_ref[0] = st[0]
            st_ref[1] = st[1]

    def pair(finalize):
        def body(i, carry):
            run((2 * i, 0, finalize), (2 * i + 1, 1))
            run((2 * i + 1, 1, finalize), (2 * i + 2, 0))
            return carry
        return body

    run(None, (0, 0))
    lax.fori_loop(0, half // 2, pair(False), 0)
    lax.fori_loop(half // 2, n_groups // 2 - 1, pair(True), 0)
    run((n_groups - 2, 0, True), (n_groups - 1, 1))
    run((n_groups - 1, 1, True), None)


def _hgrn(p16, p32, lb_rows, gn, ctx_len):
    bsz, lk, _ = p16.shape
    length = lk - ctx_len
    gsz = HG_GROUP * HG_CHUNK
    assert (length // gsz) % 4 == 0 and ctx_len <= gsz
    strip = lambda col: pl.BlockSpec((None, lk, LANES), lambda b, h: (b, 0, col // LANES + h))
    whole = lambda arr: pl.BlockSpec(arr.shape, lambda b, h: (0,) * arr.ndim)
    return pl.pallas_call(
        functools.partial(_hgrn_kernel, ctx_len=ctx_len, length=length),
        grid=(bsz, HEADS),
        in_specs=[strip(COL_HQ), strip(COL_HI), strip(COL_HG), strip(0), strip(HEADS * HG_DIM),
                  pl.BlockSpec((None, 4, LANES), lambda b, h: (h, 0, 0)),
                  whole(gn)],
        out_specs=pl.BlockSpec((None, length, LANES), lambda b, h: (b, 0, h)),
        out_shape=jax.ShapeDtypeStruct((bsz, length, HEADS * HG_DIM), BF16),
        scratch_shapes=[pltpu.VMEM((length, HG_DIM), F32), pltpu.VMEM((2, HG_DIM, HG_DIM), F32),
                        pltpu.VMEM((2, 2, gsz, HG_DIM), BF16), pltpu.VMEM((2, 2, gsz, HG_DIM), BF16),
                        pltpu.VMEM((2, 2, gsz, HG_DIM), BF16), pltpu.VMEM((2, 2, HG_GROUP, HG_DIM), F32)],
        compiler_params=_params(("arbitrary", "arbitrary"), 48),
        name="hgrn",
    )(p16, p16, p16, p32, p32, lb_rows, gn)


def _merge_kernel(attn_ref, hg_ref, gl_ref, x_ref, mod_ref, ng_ref, wa_ref, wh_ref, wo_ref, x1_ref, h2_ref, *, d):
    b = pl.program_id(0)
    ya = jnp.dot(attn_ref[...], wa_ref[...], preferred_element_type=F32)
    yh = jnp.dot(hg_ref[...], wh_ref[...], preferred_element_type=F32)
    g0 = jax.nn.sigmoid(gl_ref[:, 0:d].astype(F32))
    g1 = jax.nn.sigmoid(gl_ref[:, d:2 * d].astype(F32))
    z = (g0 * ya + g1 * yh).astype(BF16)
    o = jnp.dot(z, wo_ref[...], preferred_element_type=F32)
    gt_a = mod_ref[pl.ds(b, 1), 2 * d:3 * d]
    sh_f = mod_ref[pl.ds(b, 1), 3 * d:4 * d]
    sc_f = mod_ref[pl.ds(b, 1), 4 * d:5 * d]
    x1 = x_ref[...] + gt_a * (_rms(o) * ng_ref[1:2, :])
    x1_ref[...] = x1
    h2_ref[...] = (_rms(x1) * ng_ref[2:3, :] * (1.0 + sc_f) + sh_f).astype(BF16)


def _merge(attn, hgn, p16, x, mod, norm_g, w_bm, w_bh, w_o):
    bsz, length, d = x.shape
    tm = _pick(length, (256,))
    whole = lambda arr: pl.BlockSpec(arr.shape, lambda b, i: (0,) * arr.ndim)
    return pl.pallas_call(
        functools.partial(_merge_kernel, d=d),
        grid=(bsz, length // tm),
        in_specs=[pl.BlockSpec((None, tm, attn.shape[2]), lambda b, i: (b, i, 0)),
                  pl.BlockSpec((None, tm, hgn.shape[2]), lambda b, i: (b, i, 0)),
                  pl.BlockSpec((None, tm, 2 * d), lambda b, i: (b, i, COL_GATES // (2 * d))),
                  pl.BlockSpec((None, tm, d), lambda b, i: (b, i, 0)),
                  whole(mod), whole(norm_g), whole(w_bm), whole(w_bh), whole(w_o)],
        out_specs=[pl.BlockSpec((None, tm, d), lambda b, i: (b, i, 0)),
                   pl.BlockSpec((None, tm, d), lambda b, i: (b, i, 0))],
        out_shape=[jax.ShapeDtypeStruct((bsz, length, d), F32),
                   jax.ShapeDtypeStruct((bsz, length, d), BF16)],
        compiler_params=_params(("arbitrary", "arbitrary"), 56),
        name="merge",
    )(attn, hgn, p16, x, mod, norm_g, w_bm, w_bh, w_o)


def _ffn_kernel(h_ref, wa_ref, wb_ref, wo_ref, x1_ref, mod_ref, ng_ref, o_ref, acc_ref, *, d, nf):
    b = pl.program_id(0)
    f = pl.program_id(2)

    @pl.when(f == 0)
    def _():
        acc_ref[...] = jnp.zeros(acc_ref.shape, F32)

    h = h_ref[...]
    a = jnp.dot(h, wa_ref[...], preferred_element_type=F32)
    g = jnp.dot(h, wb_ref[...], preferred_element_type=F32)
    act = (_silu(a) * g).astype(BF16)
    acc_ref[...] += jnp.dot(act, wo_ref[...], preferred_element_type=F32)

    @pl.when(f == nf - 1)
    def _():
        gt_f = mod_ref[pl.ds(b, 1), 5 * d:6 * d]
        o_ref[...] = x1_ref[...] + gt_f * (_rms(acc_ref[...]) * ng_ref[3:4, :])


def _ffn(h2, x1, mod, norm_g, w_fi, w_fo):
    bsz, length, d = x1.shape
    d_ff = w_fo.shape[0]
    tm = _pick(length, (512, 256))
    tf = _pick(d_ff, (512, 256, 128))
    nf = d_ff // tf
    whole = lambda arr: pl.BlockSpec(arr.shape, lambda b, i, f: (0,) * arr.ndim)
    return pl.pallas_call(
        functools.partial(_ffn_kernel, d=d, nf=nf),
        grid=(bsz, length // tm, nf),
        in_specs=[pl.BlockSpec((None, tm, d), lambda b, i, f: (b, i, 0)),
                  pl.BlockSpec((d, tf), lambda b, i, f: (0, f)),
                  pl.BlockSpec((d, tf), lambda b, i, f: (0, nf + f)),
                  pl.BlockSpec((tf, d), lambda b, i, f: (f, 0)),
                  pl.BlockSpec((None, tm, d), lambda b, i, f: (b, i, 0)),
                  whole(mod), whole(norm_g)],
        out_specs=pl.BlockSpec((None, tm, d), lambda b, i, f: (b, i, 0)),
        out_shape=jax.ShapeDtypeStruct((bsz, length, d), F32),
        scratch_shapes=[pltpu.VMEM((tm, d), F32)],
        compiler_params=_params(("arbitrary", "arbitrary", "arbitrary"), 48),
        name="ffn",
    )(h2, w_fi, w_fi, w_fo, x1, mod, norm_g)


def _rope_tables(length, ctx_len):
    rows = length // GRID_W
    pairs = QK_ROPE // 4
    row = np.repeat(np.arange(rows, dtype=np.float64), GRID_W)
    col = np.tile(np.arange(GRID_W, dtype=np.float64), rows)
    inv = ROPE_THETA ** (-np.arange(pairs, dtype=np.float64) / pairs)
    ang = np.concatenate([row[:, None] * inv, col[:, None] * inv], axis=-1)
    rep = LANES // (QK_ROPE // 2)
    cos = np.concatenate([np.tile(np.cos(ang), (1, rep)), np.ones((ctx_len, LANES))], axis=0).astype(np.float32)
    sin = np.concatenate([np.tile(np.sin(ang), (1, rep)), np.zeros((ctx_len, LANES))], axis=0).astype(np.float32)
    sign = np.where((np.arange(LANES) % QK_ROPE) < QK_ROPE // 2, -1.0, 1.0).astype(np.float32)
    sin_signed_t = np.ascontiguousarray(sin.T * sign[:, None])
    return tuple(jnp.asarray(t) for t in (cos, sin, np.ascontiguousarray(cos.T), sin_signed_t))


def kernel(x, c, ctx, c_ctx, w_mod, b_mod, norm_g, w_in, mla_q_norm, mla_kv_norm, w_uq, w_ukv,
           hgrn_lb, hgrn_o_norm, w_br_mla, w_br_hgrn, w_out, w_ffn_in, w_ffn_out):
    bsz, length, d = x.shape
    ctx_len = ctx.shape[1]
    depth = w_mod.shape[0]
    assert depth == 1 and hgrn_lb.shape[0] == 2, "kernel is written for the depth-1 block"
    assert bsz < 8 and length % (HG_GROUP * HG_CHUNK) == 0 and ctx_len % HG_CHUNK == 0
    lk = length + ctx_len

    w16, w32 = _w_in_prep(w_in)

    wq = w_uq[0].reshape(Q_LORA, HEADS, QK_NOPE + QK_ROPE)
    wq_nope, wq_rope = wq[..., :QK_NOPE], wq[..., QK_NOPE:]
    zpad = jnp.zeros((Q_LORA, HEADS, QK_PAD - QK_NOPE - QK_ROPE), F32)
    wqm_t = jnp.concatenate([wq_nope, wq_rope, zpad], axis=-1).reshape(Q_LORA, HEADS * QK_PAD).T.astype(BF16)
    wkv = w_ukv[0].reshape(KV_LORA, HEADS, QK_NOPE + V_HEAD)
    wk = wkv[..., :QK_NOPE].reshape(KV_LORA, HEADS * QK_NOPE).astype(BF16)
    wvt = wkv[..., QK_NOPE:].reshape(KV_LORA, HEADS * V_HEAD).T.astype(BF16)
    w_bm = w_br_mla[0].astype(BF16)
    w_bh = w_br_hgrn[0].astype(BF16)
    w_o = w_out[0].astype(BF16)
    w_fi = w_ffn_in[0].astype(BF16)
    w_fo = w_ffn_out[0].astype(BF16)
    lb_rows = hgrn_lb.transpose(2, 0, 1, 3).reshape(HEADS, 4, HG_DIM)
    tables = _rope_tables(length, ctx_len)

    cc = jnp.concatenate([c, c_ctx[None, :], jnp.zeros((8 - bsz - 1, d), F32)], axis=0)
    mod = _mod(cc, w_mod[0], b_mod[0][None, :])
    ng = norm_g[0]

    hcat = _norm_mod(x, ctx, mod, ng)
    h2d = hcat.reshape(bsz * lk, d)
    p16 = _matmul(h2d, w16, BF16, "in_proj16").reshape(bsz, lk, N16)
    p32 = _matmul(h2d, w32, F32, "in_proj32").reshape(bsz, lk, 2 * HEADS * HG_DIM)
    qt, k, vt = _mla_prep(p16, tables, mla_q_norm[0][None, :], mla_kv_norm[0][None, :],
                          wqm_t, wk, wvt, ctx_len)
    attn = _attention(qt, k, vt)
    hgn = _hgrn(p16, p32, lb_rows, hgrn_o_norm[0][None, :], ctx_len)
    x1, h2 = _merge(attn, hgn, p16, x, mod, ng, w_bm, w_bh, w_o)
    return _ffn(h2, x1, mod, ng, w_fi, w_fo)
```

```python
import functools

import jax
import jax.numpy as jnp
import numpy as np
from jax import lax
from jax.experimental import pallas as pl
from jax.experimental.pallas import tpu as pltpu

F32 = jnp.float32
BF16 = jnp.bfloat16

RMS_EPS = 1e-6
N_MOD = 6
GRID_W = 64
ROPE_THETA = 10000.0
HEADS = 8
Q_LORA = 512
KV_LORA = 256
QK_NOPE = 128
QK_ROPE = 64
V_HEAD = 128
QK_PAD = 256
HG_DIM = 128
HG_CHUNK = 64
HG_GROUP = 8
LANES = 128
NORM_SLAB = 16
KV_TILE = 256
VT_ROWS = 144
LOG2E = 1.4426950408889634
MIB = 1024 * 1024

COL_GATES = 0
COL_HQ = 4096
COL_HI = 5120
COL_HG = 6144
COL_CQ = 7168
COL_CKV = 7680
COL_KR = 7936
N16 = 8192


def _pick(n, candidates):
    for c in candidates:
        if n % c == 0:
            return c
    raise ValueError(f"no tile for {n} in {candidates}")


def _params(sem, vmem_mib):
    return pltpu.CompilerParams(dimension_semantics=sem, vmem_limit_bytes=vmem_mib * MIB)


def _rms(xf):
    return xf * lax.rsqrt(jnp.mean(xf * xf, axis=-1, keepdims=True) + RMS_EPS)


def _silu(x):
    return x * jax.nn.sigmoid(x)


def _mod_kernel(c_ref, w_ref, b_ref, o_ref):
    s = _silu(c_ref[...]).astype(BF16)
    o_ref[...] = jnp.dot(s, w_ref[...].astype(BF16), preferred_element_type=F32) + b_ref[...]


def _mod(cc, w_mod, b_mod):
    d, n = w_mod.shape
    tn = _pick(n, (1024, 512, 256, 128))
    return pl.pallas_call(
        _mod_kernel,
        grid=(n // tn,),
        in_specs=[pl.BlockSpec((8, d), lambda j: (0, 0)),
                  pl.BlockSpec((d, tn), lambda j: (0, j)),
                  pl.BlockSpec((1, tn), lambda j: (0, j))],
        out_specs=pl.BlockSpec((8, tn), lambda j: (0, j)),
        out_shape=jax.ShapeDtypeStruct((8, n), F32),
        compiler_params=_params(("arbitrary",), 40),
        name="mod",
    )(cc, w_mod, b_mod)


def _norm_mod_kernel(src_ref, mod_ref, g_ref, *rest, d, mod_row):
    o_ref = rest[-1]
    row = pl.program_id(0) if mod_row is None else mod_row
    g = g_ref[0:1, :]
    sh = mod_ref[pl.ds(row, 1), 0:d]
    sc1 = 1.0 + mod_ref[pl.ds(row, 1), d:2 * d]

    def slab(r, carry):
        rows = pl.ds(pl.multiple_of(r * NORM_SLAB, NORM_SLAB), NORM_SLAB)
        o_ref[rows, :] = (_rms(src_ref[rows, :]) * g * sc1 + sh).astype(BF16)
        return carry

    lax.fori_loop(0, o_ref.shape[0] // NORM_SLAB, slab, 0, unroll=4)


def _norm_mod(x, ctx, mod, norm_g):
    bsz, length, d = x.shape
    ctx_len = ctx.shape[1]
    tm = _pick(length, (1024, 512, 256))
    small = [pl.BlockSpec(mod.shape, lambda b, t: (0, 0)), pl.BlockSpec(norm_g.shape, lambda b, t: (0, 0))]
    out_shape = jax.ShapeDtypeStruct((bsz, length + ctx_len, d), BF16)
    stream = pl.pallas_call(
        functools.partial(_norm_mod_kernel, d=d, mod_row=None),
        grid=(bsz, length // tm),
        in_specs=[pl.BlockSpec((None, tm, d), lambda b, t: (b, t, 0))] + small,
        out_specs=pl.BlockSpec((None, tm, d), lambda b, t: (b, t, 0)),
        out_shape=out_shape,
        compiler_params=_params(("arbitrary", "arbitrary"), 40),
        name="norm_mod",
    )(x, mod, norm_g)
    return pl.pallas_call(
        functools.partial(_norm_mod_kernel, d=d, mod_row=bsz),
        grid=(bsz, 1),
        in_specs=[pl.BlockSpec((None, ctx_len, d), lambda b, t: (b, 0, 0))] + small
                 + [pl.BlockSpec(memory_space=pl.ANY)],
        out_specs=pl.BlockSpec((None, ctx_len, d), lambda b, t: (b, length // ctx_len, 0)),
        out_shape=out_shape,
        input_output_aliases={3: 0},
        compiler_params=_params(("arbitrary", "arbitrary"), 32),
        name="norm_mod_ctx",
    )(ctx, mod, norm_g, stream)


def _w_in_kernel(w_ref, o16_ref, o32_ref, *, d):
    o = [0]
    for width in (Q_LORA, KV_LORA, QK_ROPE, 1024, 1024, 1024, 1024, 1024, 2 * d):
        o.append(o[-1] + width)
    grab = lambda lo, hi: w_ref[:, lo:hi].astype(BF16)
    o16_ref[:, COL_GATES:COL_GATES + 2 * d] = grab(o[8], o[9])
    o16_ref[:, COL_HQ:COL_HQ + 1024] = grab(o[3], o[4])
    o16_ref[:, COL_HI:COL_HI + 1024] = grab(o[6], o[7])
    o16_ref[:, COL_HG:COL_HG + 1024] = grab(o[7], o[8])
    o16_ref[:, COL_CQ:COL_KR] = grab(o[0], o[2])
    kr = grab(o[2], o[3])
    half = QK_ROPE // 2
    tail = jnp.zeros((kr.shape[0], N16 - COL_KR - 2 * QK_ROPE), BF16)
    o16_ref[:, COL_KR:N16] = jnp.concatenate([kr, -kr[:, half:], kr[:, :half], tail], axis=1)
    o32_ref[:, 0:1024] = grab(o[4], o[5])
    o32_ref[:, 1024:2048] = grab(o[5], o[6])


def _w_in_prep(w3d):
    _, k, n = w3d.shape
    tk = _pick(k, (256, 128))
    return pl.pallas_call(
        functools.partial(_w_in_kernel, d=k),
        grid=(k // tk,),
        in_specs=[pl.BlockSpec((None, tk, n), lambda i: (0, i, 0))],
        out_specs=[pl.BlockSpec((tk, N16), lambda i: (i, 0)),
                   pl.BlockSpec((tk, 2 * HEADS * HG_DIM), lambda i: (i, 0))],
        out_shape=[jax.ShapeDtypeStruct((k, N16), BF16), jax.ShapeDtypeStruct((k, 2 * HEADS * HG_DIM), BF16)],
        compiler_params=_params(("arbitrary",), 48),
        name="w_in_prep",
    )(w3d)


def _matmul_kernel(a_ref, w_ref, o_ref):
    o_ref[...] = jnp.dot(a_ref[...], w_ref[...], preferred_element_type=F32).astype(o_ref.dtype)


def _matmul(a, w, out_dtype, name):
    m, k = a.shape
    n = w.shape[1]
    tm = _pick(m, (1024, 512, 256))
    tn = _pick(n, (1024, 512, 256))
    return pl.pallas_call(
        _matmul_kernel,
        grid=(m // tm, n // tn),
        in_specs=[pl.BlockSpec((tm, k), lambda i, j: (i, 0)),
                  pl.BlockSpec((k, tn), lambda i, j: (0, j))],
        out_specs=pl.BlockSpec((tm, tn), lambda i, j: (i, j)),
        out_shape=jax.ShapeDtypeStruct((m, n), out_dtype),
        compiler_params=_params(("arbitrary", "arbitrary"), 40),
        name=name,
    )(a, w)


def _mla_prep_kernel(cq_ref, ckv_ref, kr_ref, cos_ref, sin_ref, cost_ref, sint_ref, gq_ref, gkv_ref,
                     wqmt_ref, wk_ref, wvt_ref, qt_ref, k_ref, vt_ref, *, q_scale, n_lat):
    t = pl.program_id(1)
    nt_dims = (((1,), (1,)), ((), ()))
    half = QK_ROPE // 2

    @pl.when(t < n_lat)
    def _():
        cqn = (_rms(cq_ref[...].astype(F32)) * gq_ref[...]).astype(BF16)
        main_t = lax.dot_general(wqmt_ref[...], cqn, nt_dims, preferred_element_type=F32)
        cos_t = cost_ref[...]
        sin_t = sint_ref[...]
        for h in range(HEADS):
            r0 = h * QK_PAD
            qt_ref[h, 0:QK_NOPE, :] = (main_t[r0:r0 + QK_NOPE, :] * q_scale).astype(BF16)
            rope = main_t[r0 + QK_NOPE:r0 + QK_PAD, :]
            swapped = jnp.concatenate([rope[half:2 * half], rope[0:half], rope[2 * half:]], axis=0)
            qt_ref[h, QK_NOPE:QK_PAD, :] = ((rope * cos_t + swapped * sin_t) * q_scale).astype(BF16)

    cos = cos_ref[...]
    sin = sin_ref[...]
    ckvn = (_rms(ckv_ref[...].astype(F32)) * gkv_ref[...]).astype(BF16)
    kn = jnp.dot(ckvn, wk_ref[...], preferred_element_type=F32)
    vt = lax.dot_general(wvt_ref[...], ckvn, nt_dims, preferred_element_type=F32)
    tm = vt.shape[1]
    ones_rows = (lax.broadcasted_iota(jnp.int32, (VT_ROWS - V_HEAD, tm), 0) == 0).astype(BF16)
    lane = lax.broadcasted_iota(jnp.int32, cos.shape, 1)
    a = kr_ref[...].astype(F32) * jnp.where(lane < QK_ROPE, cos, sin)
    k_rope = (a + pltpu.roll(a, QK_ROPE, axis=1)).astype(BF16)
    for h in range(HEADS):
        c0 = h * QK_PAD
        k_ref[:, c0:c0 + QK_NOPE] = kn[:, h * QK_NOPE:(h + 1) * QK_NOPE].astype(BF16)
        k_ref[:, c0 + QK_NOPE:c0 + QK_PAD] = k_rope
        vt_ref[h, 0:V_HEAD, :] = vt[h * V_HEAD:(h + 1) * V_HEAD, :].astype(BF16)
        vt_ref[h, V_HEAD:VT_ROWS, :] = ones_rows


def _mla_prep(p16, tables, gq, gkv, wqm_t, wk, wvt, ctx_len):
    bsz, lk, _ = p16.shape
    tm = KV_TILE
    assert ctx_len == tm
    nt = lk // tm
    n_lat = nt - 1
    cos_t, sin_t, cos_tt, sin_tt = tables
    whole = lambda arr: pl.BlockSpec(arr.shape, lambda b, t: (0,) * arr.ndim)
    return pl.pallas_call(
        functools.partial(_mla_prep_kernel, q_scale=float((QK_NOPE + QK_ROPE) ** -0.5 * LOG2E), n_lat=n_lat),
        grid=(bsz, nt),
        in_specs=[pl.BlockSpec((None, tm, Q_LORA), lambda b, t: (b, t, COL_CQ // Q_LORA)),
                  pl.BlockSpec((None, tm, KV_LORA), lambda b, t: (b, t, COL_CKV // KV_LORA)),
                  pl.BlockSpec((None, tm, LANES), lambda b, t: (b, t, COL_KR // LANES)),
                  pl.BlockSpec((tm, LANES), lambda b, t: (t, 0)),
                  pl.BlockSpec((tm, LANES), lambda b, t: (t, 0)),
                  pl.BlockSpec((LANES, tm), lambda b, t: (0, t)),
                  pl.BlockSpec((LANES, tm), lambda b, t: (0, t)),
                  whole(gq), whole(gkv), whole(wqm_t), whole(wk), whole(wvt)],
        out_specs=[pl.BlockSpec((None, HEADS, None, QK_PAD, tm), lambda b, t: (b, 0, jnp.minimum(t, n_lat - 1), 0, 0)),
                   pl.BlockSpec((None, tm, HEADS * QK_PAD), lambda b, t: (b, t, 0)),
                   pl.BlockSpec((None, HEADS, None, VT_ROWS, tm), lambda b, t: (b, 0, t, 0, 0))],
        out_shape=[jax.ShapeDtypeStruct((bsz, HEADS, n_lat, QK_PAD, tm), BF16),
                   jax.ShapeDtypeStruct((bsz, lk, HEADS * QK_PAD), BF16),
                   jax.ShapeDtypeStruct((bsz, HEADS, nt, VT_ROWS, tm), BF16)],
        compiler_params=_params(("arbitrary", "arbitrary"), 40),
        name="mla_prep",
    )(p16, p16, p16, cos_t, sin_t, cos_tt, sin_tt, gq, gkv, wqm_t, wk, wvt)


def _attn_kernel(qt_ref, k_ref, vt_ref, o_ref, s_ref, acc_ref, *, tq, kt, nk):
    tk = kt * KV_TILE
    sub = tq // KV_TILE
    nq = qt_ref.shape[0] // sub
    q_cols = [slice(h * KV_TILE, (h + 1) * KV_TILE) for h in range(sub)]
    acc_ref[...] = jnp.zeros(acc_ref.shape, F32)

    def scores(i, j, slot):
        r0 = pl.multiple_of(j * tk, tk)
        ks = k_ref[pl.ds(r0, tk), :]
        cmax = []
        for h, qs in enumerate(q_cols):
            s = jnp.dot(ks, qt_ref[i * sub + h], preferred_element_type=F32)
            s_ref[slot, :, qs] = s
            cmax.append(jnp.max(s, axis=0, keepdims=True))
        return jnp.concatenate(cmax, axis=1)

    def accumulate(j, slot, m_old, cmax):
        m_new = jnp.maximum(m_old, cmax)
        alpha = jnp.exp2(m_old - m_new)
        for qs in q_cols:
            pv = None
            for c in range(kt):
                p = jnp.exp2(s_ref[slot, c * KV_TILE:(c + 1) * KV_TILE, qs] - m_new[:, qs]).astype(BF16)
                d = jnp.dot(vt_ref[j * kt + c], p, preferred_element_type=F32)
                pv = d if pv is None else pv + d
            acc_ref[:, qs] = alpha[:, qs] * acc_ref[:, qs] + pv
        return m_new

    neg_inf = jnp.full((1, tq), -jnp.inf, F32)

    def tile(i, cm0):
        def triple(t, carry):
            m, cm0 = carry
            cm1 = scores(i, 3 * t + 1, 1)
            m = accumulate(3 * t, 0, m, cm0)
            cm2 = scores(i, 3 * t + 2, 2)
            m = accumulate(3 * t + 1, 1, m, cm1)
            cm0 = scores(i, 3 * t + 3, 0)
            m = accumulate(3 * t + 2, 2, m, cm2)
            return m, cm0

        m, cm0 = lax.fori_loop(0, (nk - 2) // 3, triple, (neg_inf, cm0))
        cm1 = scores(i, nk - 1, 1)
        m = accumulate(nk - 2, 0, m, cm0)
        cm_next = scores(jnp.minimum(i + 1, nq - 1), 0, 0)
        accumulate(nk - 1, 1, m, cm1)
        acc = acc_ref[...]
        o = acc[0:V_HEAD, :] / acc[V_HEAD:V_HEAD + 1, :]
        o_ref[pl.ds(pl.multiple_of(i * tq, tq), tq), :] = jnp.transpose(o.astype(o_ref.dtype))
        return cm_next

    lax.fori_loop(0, nq, tile, scores(0, 0, 0))


def _attention(qt, k, vt):
    bsz, heads, n_lat, _, _ = qt.shape
    length = n_lat * KV_TILE
    lk = k.shape[1]
    nc = vt.shape[2]
    tq = _pick(length, (1024,))
    kt = _pick(nc, (3, 1))
    assert (nc // kt) % 3 == 2, "the three-slot chunk pipeline peels two trailing chunks per query tile"
    return pl.pallas_call(
        functools.partial(_attn_kernel, tq=tq, kt=kt, nk=nc // kt),
        grid=(bsz, heads),
        in_specs=[pl.BlockSpec((None, None, n_lat, QK_PAD, KV_TILE), lambda b, h: (b, h, 0, 0, 0)),
                  pl.BlockSpec((None, lk, QK_PAD), lambda b, h: (b, 0, h)),
                  pl.BlockSpec((None, None, nc, VT_ROWS, KV_TILE), lambda b, h: (b, h, 0, 0, 0))],
        out_specs=pl.BlockSpec((None, length, V_HEAD), lambda b, h: (b, 0, h)),
        out_shape=jax.ShapeDtypeStruct((bsz, length, heads * V_HEAD), BF16),
        scratch_shapes=[pltpu.VMEM((3, kt * KV_TILE, tq), F32), pltpu.VMEM((VT_ROWS, tq), F32)],
        compiler_params=_params(("arbitrary", "arbitrary"), 48),
        name="attn",
    )(qt, k, vt)


def _hgrn_kernel(hq_ref, hi_ref, hg_ref, ff_ref, fb_ref, lb_ref, gn_ref, o_ref,
                 oacc_ref, st_ref, qd_ref, kd_ref, kt_ref, dec_ref, *, ctx_len, length):
    c = HG_CHUNK
    gsz = HG_GROUP * c
    n_groups = length // gsz
    half = n_groups // 2
    row = lax.broadcasted_iota(jnp.int32, (c, c), 0)
    col = lax.broadcasted_iota(jnp.int32, (c, c), 1)
    tri_c = (row >= col, row <= col)
    lbr = lb_ref[...]
    q_scale = float(HG_DIM ** -0.5)
    nt_dims = (((1,), (1,)), ((), ()))
    tn_dims = (((0,), (0,)), ((), ()))

    def lower_bound(direction):
        a0 = lbr[direction:direction + 1, :]
        a1 = lbr[2 + direction:3 + direction, :]
        mx = jnp.maximum(a0, a1)
        e0 = jnp.exp(a0 - mx)
        e1 = jnp.exp(a1 - mx)
        return e0 / (e0 + e1)

    lbs = (lower_bound(0), lower_bound(1))

    def chunk_order(direction, nchunks):
        return range(nchunks) if direction == 0 else range(nchunks - 1, -1, -1)

    def gates(direction, r0, nchunks):
        n = nchunks * c
        f_ref = ff_ref if direction == 0 else fb_ref
        lb = lbs[direction]
        f = lb + (1.0 - lb) * jax.nn.sigmoid(f_ref[pl.ds(r0, n), :])
        kk = 1.0 - f
        lf = jnp.log(f) * LOG2E
        pos = lax.broadcasted_iota(jnp.int32, (n, HG_DIM), 0) & (c - 1)
        bcum = lf
        for sh in (1, 2, 4, 8, 16, 32):
            if direction == 0:
                bcum = bcum + jnp.where(pos >= sh, pltpu.roll(bcum, sh, axis=0), 0.0)
            else:
                bcum = bcum + jnp.where(pos < c - sh, pltpu.roll(bcum, n - sh, axis=0), 0.0)
        last = c - 1 if direction == 0 else 0
        b_last = [bcum[j * c + last:j * c + last + 1, :] for j in range(nchunks)]
        b_last_rows = jnp.concatenate([jnp.broadcast_to(bl, (c, HG_DIM)) for bl in b_last], axis=0)
        k_tail = (kk * jnp.exp2(b_last_rows - bcum)).astype(BF16)
        return kk, bcum, k_tail, jnp.exp2(jnp.concatenate(b_last, axis=0))

    def state_update(st, v, k_tail, decay_row):
        u_t = lax.dot_general(v, k_tail, tn_dims, preferred_element_type=F32)
        return st * decay_row + u_t

    for direction in (0, 1):
        nchunks = ctx_len // c
        _, _, k_tail, decay = gates(direction, length, nchunks)
        st = jnp.zeros((HG_DIM, HG_DIM), F32)
        for j in chunk_order(direction, nchunks):
            sl = slice(j * c, (j + 1) * c)
            st = state_update(st, hi_ref[length + j * c:length + (j + 1) * c, :], k_tail[sl], decay[j:j + 1, :])
        st_ref[direction] = st

    def groups_of(s):
        return (s, n_groups - 1 - s)

    pc = 2
    pr = pc * c
    units = HG_GROUP // pc

    def prepare_unit(s, slot, direction, u):
        g = groups_of(s)[direction]
        r0 = pl.multiple_of(g * gsz + u * pr, pr)
        rows = slice(u * pr, (u + 1) * pr)
        kk, bcum, k_tail, decay = gates(direction, r0, pc)
        q = _silu(hq_ref[pl.ds(r0, pr), :].astype(F32)) * q_scale
        qd_ref[slot, direction, rows, :] = (q * jnp.exp2(bcum)).astype(BF16)
        kd_ref[slot, direction, rows, :] = (kk * jnp.exp2(-bcum)).astype(BF16)
        kt_ref[slot, direction, rows, :] = k_tail
        dec_ref[slot, direction, u * pc:(u + 1) * pc, :] = decay

    def consume_unit(s, slot, direction, u, st, finalize):
        g = groups_of(s)[direction]
        o0 = pl.multiple_of(g * gsz + u * pr, pr)
        outs = [None] * pc
        for jj in chunk_order(direction, pc):
            j = u * pc + jj
            sl = slice(j * c, (j + 1) * c)
            q_d = qd_ref[slot, direction, sl, :]
            v = hi_ref[pl.ds(o0 + jj * c, c), :]
            sc = lax.dot_general(q_d, kd_ref[slot, direction, sl, :], nt_dims, preferred_element_type=F32)
            sc = jnp.where(tri_c[direction], sc, 0.0).astype(BF16)
            outs[jj] = (jnp.dot(sc, v, preferred_element_type=F32)
                        + lax.dot_general(q_d, st.astype(BF16), nt_dims, preferred_element_type=F32))
            st = state_update(st, v, kt_ref[slot, direction, sl, :], dec_ref[slot, direction, j:j + 1, :])
        out = jnp.concatenate(outs, axis=0)
        if finalize:
            tot = oacc_ref[pl.ds(o0, pr), :] + out
            gate = _silu(hg_ref[pl.ds(o0, pr), :].astype(F32))
            o_ref[pl.ds(o0, pr), :] = (_rms(tot) * gn_ref[...] * gate).astype(o_ref.dtype)
        else:
            oacc_ref[pl.ds(o0, pr), :] = out
        return st

    def run(cons, prep):
        st = [st_ref[0], st_ref[1]] if cons is not None else None
        for idx in range(units):
            for direction in (0, 1):
                u = chunk_order(direction, units)[idx]
                if cons is not None:
                    st[direction] = consume_unit(cons[0], cons[1], direction, u, st[direction], cons[2])
                if prep is not None:
                    prepare_unit(prep[0], prep[1], direction, u)
        if cons is not None:
            st_ref[0] = st[0]
            st_ref[1] = st[1]

    def pair(finalize):
        def body(i, carry):
            run((2 * i, 0, finalize), (2 * i + 1, 1))
            run((2 * i + 1, 1, finalize), (2 * i + 2, 0))
            return carry
        return body

    run(None, (0, 0))
    lax.fori_loop(0, half // 2, pair(False), 0)
    lax.fori_loop(half // 2, n_groups // 2 - 1, pair(True), 0)
    run((n_groups - 2, 0, True), (n_groups - 1, 1))
    run((n_groups - 1, 1, True), None)


def _hgrn(p16, p32, lb_rows, gn, ctx_len):
    bsz, lk, _ = p16.shape
    length = lk - ctx_len
    gsz = HG_GROUP * HG_CHUNK
    assert (length // gsz) % 4 == 0 and ctx_len <= gsz
    strip = lambda col: pl.BlockSpec((None, lk, LANES), lambda b, h: (b, 0, col // LANES + h))
    whole = lambda arr: pl.BlockSpec(arr.shape, lambda b, h: (0,) * arr.ndim)
    return pl.pallas_call(
        functools.partial(_hgrn_kernel, ctx_len=ctx_len, length=length),
        grid=(bsz, HEADS),
        in_specs=[strip(COL_HQ), strip(COL_HI), strip(COL_HG), strip(0), strip(HEADS * HG_DIM),
                  pl.BlockSpec((None, 4, LANES), lambda b, h: (h, 0, 0)),
                  whole(gn)],
        out_specs=pl.BlockSpec((None, length, LANES), lambda b, h: (b, 0, h)),
        out_shape=jax.ShapeDtypeStruct((bsz, length, HEADS * HG_DIM), BF16),
        scratch_shapes=[pltpu.VMEM((length, HG_DIM), F32), pltpu.VMEM((2, HG_DIM, HG_DIM), F32),
                        pltpu.VMEM((2, 2, gsz, HG_DIM), BF16), pltpu.VMEM((2, 2, gsz, HG_DIM), BF16),
                        pltpu.VMEM((2, 2, gsz, HG_DIM), BF16), pltpu.VMEM((2, 2, HG_GROUP, HG_DIM), F32)],
        compiler_params=_params(("arbitrary", "arbitrary"), 48),
        name="hgrn",
    )(p16, p16, p16, p32, p32, lb_rows, gn)


def _merge_kernel(attn_ref, hg_ref, gl_ref, x_ref, mod_ref, ng_ref, wa_ref, wh_ref, wo_ref, x1_ref, h2_ref, *, d):
    b = pl.program_id(0)
    gt_a = mod_ref[pl.ds(b, 1), 2 * d:3 * d]
    sh_f = mod_ref[pl.ds(b, 1), 3 * d:4 * d]
    sc_f = mod_ref[pl.ds(b, 1), 4 * d:5 * d]
    gain_a = ng_ref[1:2, :] * gt_a
    gain_f = ng_ref[2:3, :] * (1.0 + sc_f)
    rows_half = attn_ref.shape[0] // 2
    for r0 in (0, rows_half):
        rs = slice(r0, r0 + rows_half)
        ya = jnp.dot(attn_ref[rs, :], wa_ref[...], preferred_element_type=F32)
        yh = jnp.dot(hg_ref[rs, :], wh_ref[...], preferred_element_type=F32)
        g0 = jax.nn.sigmoid(gl_ref[rs, 0:d].astype(F32))
        g1 = jax.nn.sigmoid(gl_ref[rs, d:2 * d].astype(F32))
        z = (g0 * ya + g1 * yh).astype(BF16)
        o = jnp.dot(z, wo_ref[...], preferred_element_type=F32)
        x1 = x_ref[rs, :] + _rms(o) * gain_a
        x1_ref[rs, :] = x1
        h2_ref[rs, :] = (_rms(x1) * gain_f + sh_f).astype(BF16)


def _merge(attn, hgn, p16, x, mod, norm_g, w_bm, w_bh, w_o):
    bsz, length, d = x.shape
    tm = _pick(length, (512, 256))
    whole = lambda arr: pl.BlockSpec(arr.shape, lambda b, i: (0,) * arr.ndim, pipeline_mode=pl.Buffered(1))
    return pl.pallas_call(
        functools.partial(_merge_kernel, d=d),
        grid=(bsz, length // tm),
        in_specs=[pl.BlockSpec((None, tm, attn.shape[2]), lambda b, i: (b, i, 0)),
                  pl.BlockSpec((None, tm, hgn.shape[2]), lambda b, i: (b, i, 0)),
                  pl.BlockSpec((None, tm, 2 * d), lambda b, i: (b, i, COL_GATES // (2 * d))),
                  pl.BlockSpec((None, tm, d), lambda b, i: (b, i, 0)),
                  whole(mod), whole(norm_g), whole(w_bm), whole(w_bh), whole(w_o)],
        out_specs=[pl.BlockSpec((None, tm, d), lambda b, i: (b, i, 0)),
                   pl.BlockSpec((None, tm, d), lambda b, i: (b, i, 0))],
        out_shape=[jax.ShapeDtypeStruct((bsz, length, d), F32),
                   jax.ShapeDtypeStruct((bsz, length, d), BF16)],
        compiler_params=_params(("arbitrary", "arbitrary"), 56),
        name="merge",
    )(attn, hgn, p16, x, mod, norm_g, w_bm, w_bh, w_o)


def _ffn_kernel(h_ref, wa_ref, wb_ref, wo_ref, x1_ref, mod_ref, ng_ref, o_ref, acc_ref, *, d, nf):
    b = pl.program_id(0)
    f = pl.program_id(2)

    @pl.when(f == 0)
    def _():
        acc_ref[...] = jnp.zeros(acc_ref.shape, F32)

    h = h_ref[...]
    a = jnp.dot(h, wa_ref[...], preferred_element_type=F32)
    g = jnp.dot(h, wb_ref[...], preferred_element_type=F32)
    act = (_silu(a) * g).astype(BF16)
    acc_ref[...] += jnp.dot(act, wo_ref[...], preferred_element_type=F32)

    @pl.when(f == nf - 1)
    def _():
        gt_f = mod_ref[pl.ds(b, 1), 5 * d:6 * d]
        o_ref[...] = x1_ref[...] + _rms(acc_ref[...]) * (ng_ref[3:4, :] * gt_f)


def _ffn(h2, x1, mod, norm_g, w_fi, w_fo):
    bsz, length, d = x1.shape
    d_ff = w_fo.shape[0]
    tm = _pick(length, (512, 256))
    tf = _pick(d_ff, (512, 256, 128))
    nf = d_ff // tf
    whole = lambda arr: pl.BlockSpec(arr.shape, lambda b, i, f: (0,) * arr.ndim)
    return pl.pallas_call(
        functools.partial(_ffn_kernel, d=d, nf=nf),
        grid=(bsz, length // tm, nf),
        in_specs=[pl.BlockSpec((None, tm, d), lambda b, i, f: (b, i, 0)),
                  pl.BlockSpec((d, tf), lambda b, i, f: (0, f)),
                  pl.BlockSpec((d, tf), lambda b, i, f: (0, nf + f)),
                  pl.BlockSpec((tf, d), lambda b, i, f: (f, 0)),
                  pl.BlockSpec((None, tm, d), lambda b, i, f: (b, i, 0)),
                  whole(mod), whole(norm_g)],
        out_specs=pl.BlockSpec((None, tm, d), lambda b, i, f: (b, i, 0)),
        out_shape=jax.ShapeDtypeStruct((bsz, length, d), F32),
        scratch_shapes=[pltpu.VMEM((tm, d), F32)],
        compiler_params=_params(("arbitrary", "arbitrary", "arbitrary"), 48),
        name="ffn",
    )(h2, w_fi, w_fi, w_fo, x1, mod, norm_g)


def _rope_tables(length, ctx_len):
    rows = length // GRID_W
    pairs = QK_ROPE // 4
    row = np.repeat(np.arange(rows, dtype=np.float64), GRID_W)
    col = np.tile(np.arange(GRID_W, dtype=np.float64), rows)
    inv = ROPE_THETA ** (-np.arange(pairs, dtype=np.float64) / pairs)
    ang = np.concatenate([row[:, None] * inv, col[:, None] * inv], axis=-1)
    rep = LANES // (QK_ROPE // 2)
    cos = np.concatenate([np.tile(np.cos(ang), (1, rep)), np.ones((ctx_len, LANES))], axis=0).astype(np.float32)
    sin = np.concatenate([np.tile(np.sin(ang), (1, rep)), np.zeros((ctx_len, LANES))], axis=0).astype(np.float32)
    sign = np.where((np.arange(LANES) % QK_ROPE) < QK_ROPE // 2, -1.0, 1.0).astype(np.float32)
    sin_signed_t = np.ascontiguousarray(sin.T * sign[:, None])
    return tuple(jnp.asarray(t) for t in (cos, sin, np.ascontiguousarray(cos.T), sin_signed_t))


def kernel(x, c, ctx, c_ctx, w_mod, b_mod, norm_g, w_in, mla_q_norm, mla_kv_norm, w_uq, w_ukv,
           hgrn_lb, hgrn_o_norm, w_br_mla, w_br_hgrn, w_out, w_ffn_in, w_ffn_out):
    bsz, length, d = x.shape
    ctx_len = ctx.shape[1]
    depth = w_mod.shape[0]
    assert depth == 1 and hgrn_lb.shape[0] == 2, "kernel is written for the depth-1 block"
    assert bsz < 8 and length % (HG_GROUP * HG_CHUNK) == 0 and ctx_len % HG_CHUNK == 0
    lk = length + ctx_len

    w16, w32 = _w_in_prep(w_in)

    wq = w_uq[0].reshape(Q_LORA, HEADS, QK_NOPE + QK_ROPE)
    wq_nope, wq_rope = wq[..., :QK_NOPE], wq[..., QK_NOPE:]
    zpad = jnp.zeros((Q_LORA, HEADS, QK_PAD - QK_NOPE - QK_ROPE), F32)
    wqm_t = jnp.concatenate([wq_nope, wq_rope, zpad], axis=-1).reshape(Q_LORA, HEADS * QK_PAD).T.astype(BF16)
    wkv = w_ukv[0].reshape(KV_LORA, HEADS, QK_NOPE + V_HEAD)
    wk = wkv[..., :QK_NOPE].reshape(KV_LORA, HEADS * QK_NOPE).astype(BF16)
    wvt = wkv[..., QK_NOPE:].reshape(KV_LORA, HEADS * V_HEAD).T.astype(BF16)
    w_bm = w_br_mla[0].astype(BF16)
    w_bh = w_br_hgrn[0].astype(BF16)
    w_o = w_out[0].astype(BF16)
    w_fi = w_ffn_in[0].astype(BF16)
    w_fo = w_ffn_out[0].astype(BF16)
    lb_rows = hgrn_lb.transpose(2, 0, 1, 3).reshape(HEADS, 4, HG_DIM)
    tables = _rope_tables(length, ctx_len)

    cc = jnp.concatenate([c, c_ctx[None, :], jnp.zeros((8 - bsz - 1, d), F32)], axis=0)
    mod = _mod(cc, w_mod[0], b_mod[0][None, :])
    ng = norm_g[0]

    hcat = _norm_mod(x, ctx, mod, ng)
    h2d = hcat.reshape(bsz * lk, d)
    p16 = _matmul(h2d, w16, BF16, "in_proj16").reshape(bsz, lk, N16)
    p32 = _matmul(h2d, w32, F32, "in_proj32").reshape(bsz, lk, 2 * HEADS * HG_DIM)
    qt, k, vt = _mla_prep(p16, tables, mla_q_norm[0][None, :], mla_kv_norm[0][None, :],
                          wqm_t, wk, wvt, ctx_len)
    attn = _attention(qt, k, vt)
    hgn = _hgrn(p16, p32, lb_rows, hgrn_o_norm[0][None, :], ctx_len)
    x1, h2 = _merge(attn, hgn, p16, x, mod, ng, w_bm, w_bh, w_o)
    return _ffn(h2, x1, mod, ng, w_fi, w_fo)
```

```python
import functools

import jax
import jax.numpy as jnp
import numpy as np
from jax import lax
from jax.experimental import pallas as pl
from jax.experimental.pallas import tpu as pltpu

F32 = jnp.float32
BF16 = jnp.bfloat16

RMS_EPS = 1e-6
N_MOD = 6
GRID_W = 64
ROPE_THETA = 10000.0
HEADS = 8
Q_LORA = 512
KV_LORA = 256
QK_NOPE = 128
QK_ROPE = 64
V_HEAD = 128
QK_PAD = 256
HG_DIM = 128
HG_CHUNK = 64
HG_GROUP = 8
LANES = 128
NORM_SLAB = 16
KV_TILE = 256
VT_ROWS = 144
LOG2E = 1.4426950408889634
MIB = 1024 * 1024

COL_GATES = 0
COL_HQ = 4096
COL_HI = 5120
COL_HG = 6144
COL_CQ = 7168
COL_CKV = 7680
COL_KR = 7936
N16 = 8192


def _pick(n, candidates):
    for c in candidates:
        if n % c == 0:
            return c
    raise ValueError(f"no tile for {n} in {candidates}")


def _params(sem, vmem_mib):
    return pltpu.CompilerParams(dimension_semantics=sem, vmem_limit_bytes=vmem_mib * MIB)


def _rms(xf):
    return xf * lax.rsqrt(jnp.mean(xf * xf, axis=-1, keepdims=True) + RMS_EPS)


def _silu(x):
    return x * jax.nn.sigmoid(x)


def _mod_kernel(c_ref, w_ref, b_ref, o_ref):
    s = _silu(c_ref[...]).astype(BF16)
    o_ref[...] = jnp.dot(s, w_ref[...].astype(BF16), preferred_element_type=F32) + b_ref[...]


def _mod(cc, w_mod, b_mod):
    d, n = w_mod.shape
    tn = _pick(n, (1024, 512, 256, 128))
    return pl.pallas_call(
        _mod_kernel,
        grid=(n // tn,),
        in_specs=[pl.BlockSpec((8, d), lambda j: (0, 0)),
                  pl.BlockSpec((d, tn), lambda j: (0, j)),
                  pl.BlockSpec((1, tn), lambda j: (0, j))],
        out_specs=pl.BlockSpec((8, tn), lambda j: (0, j)),
        out_shape=jax.ShapeDtypeStruct((8, n), F32),
        compiler_params=_params(("arbitrary",), 40),
        name="mod",
    )(cc, w_mod, b_mod)


def _norm_mod_kernel(src_ref, mod_ref, g_ref, *rest, d, mod_row):
    o_ref = rest[-1]
    row = pl.program_id(0) if mod_row is None else mod_row
    g = g_ref[0:1, :]
    sh = mod_ref[pl.ds(row, 1), 0:d]
    sc1 = 1.0 + mod_ref[pl.ds(row, 1), d:2 * d]

    def slab(r, carry):
        rows = pl.ds(pl.multiple_of(r * NORM_SLAB, NORM_SLAB), NORM_SLAB)
        o_ref[rows, :] = (_rms(src_ref[rows, :]) * g * sc1 + sh).astype(BF16)
        return carry

    lax.fori_loop(0, o_ref.shape[0] // NORM_SLAB, slab, 0, unroll=4)


def _norm_mod(x, ctx, mod, norm_g):
    bsz, length, d = x.shape
    ctx_len = ctx.shape[1]
    tm = _pick(length, (1024, 512, 256))
    small = [pl.BlockSpec(mod.shape, lambda b, t: (0, 0)), pl.BlockSpec(norm_g.shape, lambda b, t: (0, 0))]
    out_shape = jax.ShapeDtypeStruct((bsz, length + ctx_len, d), BF16)
    stream = pl.pallas_call(
        functools.partial(_norm_mod_kernel, d=d, mod_row=None),
        grid=(bsz, length // tm),
        in_specs=[pl.BlockSpec((None, tm, d), lambda b, t: (b, t, 0))] + small,
        out_specs=pl.BlockSpec((None, tm, d), lambda b, t: (b, t, 0)),
        out_shape=out_shape,
        compiler_params=_params(("arbitrary", "arbitrary"), 40),
        name="norm_mod",
    )(x, mod, norm_g)
    return pl.pallas_call(
        functools.partial(_norm_mod_kernel, d=d, mod_row=bsz),
        grid=(bsz, 1),
        in_specs=[pl.BlockSpec((None, ctx_len, d), lambda b, t: (b, 0, 0))] + small
                 + [pl.BlockSpec(memory_space=pl.ANY)],
        out_specs=pl.BlockSpec((None, ctx_len, d), lambda b, t: (b, length // ctx_len, 0)),
        out_shape=out_shape,
        input_output_aliases={3: 0},
        compiler_params=_params(("arbitrary", "arbitrary"), 32),
        name="norm_mod_ctx",
    )(ctx, mod, norm_g, stream)


def _w_in_kernel(wt_ref, o16_ref, o32_ref, *, d):
    o = [0]
    for width in (Q_LORA, KV_LORA, QK_ROPE, 1024, 1024, 1024, 1024, 1024, 2 * d):
        o.append(o[-1] + width)

    def grab(lo, hi):
        return jnp.transpose(wt_ref[lo:hi, :]).astype(BF16)

    step = 1024
    for c in range(0, 2 * d, step):
        o16_ref[:, COL_GATES + c:COL_GATES + c + step] = grab(o[8] + c, o[8] + c + step)
    o16_ref[:, COL_HQ:COL_HQ + 1024] = grab(o[3], o[4])
    o16_ref[:, COL_HI:COL_HI + 1024] = grab(o[6], o[7])
    o16_ref[:, COL_HG:COL_HG + 1024] = grab(o[7], o[8])
    o16_ref[:, COL_CQ:COL_KR] = grab(o[0], o[2])
    kr = grab(o[2], o[2] + LANES)[:, 0:QK_ROPE]
    half = QK_ROPE // 2
    tail = jnp.zeros((kr.shape[0], N16 - COL_KR - 2 * QK_ROPE), BF16)
    o16_ref[:, COL_KR:N16] = jnp.concatenate([kr, -kr[:, half:], kr[:, :half], tail], axis=1)
    o32_ref[:, 0:1024] = grab(o[4], o[5])
    o32_ref[:, 1024:2048] = grab(o[5], o[6])


def _w_in_prep(w3d_t):
    _, n, k = w3d_t.shape
    tk = _pick(k, (256, 128))
    return pl.pallas_call(
        functools.partial(_w_in_kernel, d=k),
        grid=(k // tk,),
        in_specs=[pl.BlockSpec((None, n, tk), lambda i: (0, 0, i))],
        out_specs=[pl.BlockSpec((tk, N16), lambda i: (i, 0)),
                   pl.BlockSpec((tk, 2 * HEADS * HG_DIM), lambda i: (i, 0))],
        out_shape=[jax.ShapeDtypeStruct((k, N16), BF16), jax.ShapeDtypeStruct((k, 2 * HEADS * HG_DIM), BF16)],
        compiler_params=_params(("arbitrary",), 48),
        name="w_in_prep",
    )(w3d_t)


def _matmul_kernel(a_ref, w_ref, o_ref):
    o_ref[...] = jnp.dot(a_ref[...], w_ref[...], preferred_element_type=F32).astype(o_ref.dtype)


def _matmul(a, w, out_dtype, name):
    m, k = a.shape
    n = w.shape[1]
    tm = _pick(m, (1024, 512, 256))
    tn = _pick(n, (2048, 1024, 512, 256))
    return pl.pallas_call(
        _matmul_kernel,
        grid=(m // tm, n // tn),
        in_specs=[pl.BlockSpec((tm, k), lambda i, j: (i, 0)),
                  pl.BlockSpec((k, tn), lambda i, j: (0, j))],
        out_specs=pl.BlockSpec((tm, tn), lambda i, j: (i, j)),
        out_shape=jax.ShapeDtypeStruct((m, n), out_dtype),
        compiler_params=_params(("arbitrary", "arbitrary"), 48),
        name=name,
    )(a, w)


def _mla_prep_kernel(cq_ref, ckv_ref, kr_ref, cos_ref, sin_ref, cost_ref, sint_ref, gq_ref, gkv_ref,
                     wqmt_ref, wk_ref, wvt_ref, qt_ref, k_ref, vt_ref, *, q_scale, n_lat):
    t = pl.program_id(1)
    nt_dims = (((1,), (1,)), ((), ()))
    half = QK_ROPE // 2

    @pl.when(t < n_lat)
    def _():
        cqn = (_rms(cq_ref[...].astype(F32)) * gq_ref[...]).astype(BF16)
        main_t = lax.dot_general(wqmt_ref[...], cqn, nt_dims, preferred_element_type=F32)
        cos_t = cost_ref[...]
        sin_t = sint_ref[...]
        for h in range(HEADS):
            r0 = h * QK_PAD
            qt_ref[h, 0:QK_NOPE, :] = (main_t[r0:r0 + QK_NOPE, :] * q_scale).astype(BF16)
            rope = main_t[r0 + QK_NOPE:r0 + QK_PAD, :]
            swapped = jnp.concatenate([rope[half:2 * half], rope[0:half], rope[2 * half:]], axis=0)
            qt_ref[h, QK_NOPE:QK_PAD, :] = ((rope * cos_t + swapped * sin_t) * q_scale).astype(BF16)

    cos = cos_ref[...]
    sin = sin_ref[...]
    ckvn = (_rms(ckv_ref[...].astype(F32)) * gkv_ref[...]).astype(BF16)
    kn = jnp.dot(ckvn, wk_ref[...], preferred_element_type=F32)
    vt = lax.dot_general(wvt_ref[...], ckvn, nt_dims, preferred_element_type=F32)
    tm = vt.shape[1]
    ones_rows = (lax.broadcasted_iota(jnp.int32, (VT_ROWS - V_HEAD, tm), 0) == 0).astype(BF16)
    lane = lax.broadcasted_iota(jnp.int32, cos.shape, 1)
    a = kr_ref[...].astype(F32) * jnp.where(lane < QK_ROPE, cos, sin)
    k_rope = (a + pltpu.roll(a, QK_ROPE, axis=1)).astype(BF16)
    for h in range(HEADS):
        c0 = h * QK_PAD
        k_ref[:, c0:c0 + QK_NOPE] = kn[:, h * QK_NOPE:(h + 1) * QK_NOPE].astype(BF16)
        k_ref[:, c0 + QK_NOPE:c0 + QK_PAD] = k_rope
        vt_ref[h, 0:V_HEAD, :] = vt[h * V_HEAD:(h + 1) * V_HEAD, :].astype(BF16)
        vt_ref[h, V_HEAD:VT_ROWS, :] = ones_rows


def _mla_prep(p16, tables, gq, gkv, wqm_t, wk, wvt, ctx_len):
    bsz, lk, _ = p16.shape
    tm = KV_TILE
    assert ctx_len == tm
    nt = lk // tm
    n_lat = nt - 1
    cos_t, sin_t, cos_tt, sin_tt = tables
    whole = lambda arr: pl.BlockSpec(arr.shape, lambda b, t: (0,) * arr.ndim)
    return pl.pallas_call(
        functools.partial(_mla_prep_kernel, q_scale=float((QK_NOPE + QK_ROPE) ** -0.5 * LOG2E), n_lat=n_lat),
        grid=(bsz, nt),
        in_specs=[pl.BlockSpec((None, tm, Q_LORA), lambda b, t: (b, t, COL_CQ // Q_LORA)),
                  pl.BlockSpec((None, tm, KV_LORA), lambda b, t: (b, t, COL_CKV // KV_LORA)),
                  pl.BlockSpec((None, tm, LANES), lambda b, t: (b, t, COL_KR // LANES)),
                  pl.BlockSpec((tm, LANES), lambda b, t: (t, 0)),
                  pl.BlockSpec((tm, LANES), lambda b, t: (t, 0)),
                  pl.BlockSpec((LANES, tm), lambda b, t: (0, t)),
                  pl.BlockSpec((LANES, tm), lambda b, t: (0, t)),
                  whole(gq), whole(gkv), whole(wqm_t), whole(wk), whole(wvt)],
        out_specs=[pl.BlockSpec((None, HEADS, None, QK_PAD, tm), lambda b, t: (b, 0, jnp.minimum(t, n_lat - 1), 0, 0)),
                   pl.BlockSpec((None, tm, HEADS * QK_PAD), lambda b, t: (b, t, 0)),
                   pl.BlockSpec((None, HEADS, None, VT_ROWS, tm), lambda b, t: (b, 0, t, 0, 0))],
        out_shape=[jax.ShapeDtypeStruct((bsz, HEADS, n_lat, QK_PAD, tm), BF16),
                   jax.ShapeDtypeStruct((bsz, lk, HEADS * QK_PAD), BF16),
                   jax.ShapeDtypeStruct((bsz, HEADS, nt, VT_ROWS, tm), BF16)],
        compiler_params=_params(("arbitrary", "arbitrary"), 40),
        name="mla_prep",
    )(p16, p16, p16, cos_t, sin_t, cos_tt, sin_tt, gq, gkv, wqm_t, wk, wvt)


def _attn_kernel(qt_ref, k_ref, vt_ref, o_ref, s_ref, acc_ref, *, tq, kt, nk):
    tk = kt * KV_TILE
    sub = tq // KV_TILE
    nq = qt_ref.shape[0] // sub
    q_cols = [slice(h * KV_TILE, (h + 1) * KV_TILE) for h in range(sub)]
    acc_ref[...] = jnp.zeros(acc_ref.shape, F32)

    def scores(i, j, slot):
        r0 = pl.multiple_of(j * tk, tk)
        ks = k_ref[pl.ds(r0, tk), :]
        cmax = []
        for h, qs in enumerate(q_cols):
            s = jnp.dot(ks, qt_ref[i * sub + h], preferred_element_type=F32)
            s_ref[slot, :, qs] = s
            cmax.append(jnp.max(s, axis=0, keepdims=True))
        return jnp.concatenate(cmax, axis=1)

    def accumulate(j, slot, m_old, cmax):
        m_new = jnp.maximum(m_old, cmax)
        alpha = jnp.exp2(m_old - m_new)
        for qs in q_cols:
            pv = None
            for c in range(kt):
                p = jnp.exp2(s_ref[slot, c * KV_TILE:(c + 1) * KV_TILE, qs] - m_new[:, qs]).astype(BF16)
                d = jnp.dot(vt_ref[j * kt + c], p, preferred_element_type=F32)
                pv = d if pv is None else pv + d
            acc_ref[:, qs] = alpha[:, qs] * acc_ref[:, qs] + pv
        return m_new

    neg_inf = jnp.full((1, tq), -jnp.inf, F32)

    def tile(i, cm0):
        def triple(t, carry):
            m, cm0 = carry
            cm1 = scores(i, 3 * t + 1, 1)
            m = accumulate(3 * t, 0, m, cm0)
            cm2 = scores(i, 3 * t + 2, 2)
            m = accumulate(3 * t + 1, 1, m, cm1)
            cm0 = scores(i, 3 * t + 3, 0)
            m = accumulate(3 * t + 2, 2, m, cm2)
            return m, cm0

        m, cm0 = lax.fori_loop(0, (nk - 2) // 3, triple, (neg_inf, cm0), unroll=True)
        cm1 = scores(i, nk - 1, 1)
        m = accumulate(nk - 2, 0, m, cm0)
        cm_next = scores(jnp.minimum(i + 1, nq - 1), 0, 0)
        accumulate(nk - 1, 1, m, cm1)
        acc = acc_ref[...]
        o = acc[0:V_HEAD, :] / acc[V_HEAD:V_HEAD + 1, :]
        o_ref[pl.ds(pl.multiple_of(i * tq, tq), tq), :] = jnp.transpose(o.astype(o_ref.dtype))
        return cm_next

    lax.fori_loop(0, nq, tile, scores(0, 0, 0))


def _attention(qt, k, vt):
    bsz, heads, n_lat, _, _ = qt.shape
    length = n_lat * KV_TILE
    lk = k.shape[1]
    nc = vt.shape[2]
    tq = _pick(length, (1024,))
    kt = _pick(nc, (3, 1))
    assert (nc // kt) % 3 == 2, "the three-slot chunk pipeline peels two trailing chunks per query tile"
    return pl.pallas_call(
        functools.partial(_attn_kernel, tq=tq, kt=kt, nk=nc // kt),
        grid=(bsz, heads),
        in_specs=[pl.BlockSpec((None, None, n_lat, QK_PAD, KV_TILE), lambda b, h: (b, h, 0, 0, 0)),
                  pl.BlockSpec((None, lk, QK_PAD), lambda b, h: (b, 0, h)),
                  pl.BlockSpec((None, None, nc, VT_ROWS, KV_TILE), lambda b, h: (b, h, 0, 0, 0))],
        out_specs=pl.BlockSpec((None, length, V_HEAD), lambda b, h: (b, 0, h)),
        out_shape=jax.ShapeDtypeStruct((bsz, length, heads * V_HEAD), BF16),
        scratch_shapes=[pltpu.VMEM((3, kt * KV_TILE, tq), F32), pltpu.VMEM((VT_ROWS, tq), F32)],
        compiler_params=_params(("arbitrary", "arbitrary"), 48),
        name="attn",
    )(qt, k, vt)


def _hgrn_kernel(hq_ref, hi_ref, hg_ref, ff_ref, fb_ref, lb_ref, gn_ref, o_ref,
                 oacc_ref, st_ref, qd_ref, kd_ref, kt_ref, dec_ref, *, ctx_len, length):
    c = HG_CHUNK
    gsz = HG_GROUP * c
    n_groups = length // gsz
    half = n_groups // 2
    row = lax.broadcasted_iota(jnp.int32, (c, c), 0)
    col = lax.broadcasted_iota(jnp.int32, (c, c), 1)
    tri_c = (row >= col, row <= col)
    lbr = lb_ref[...]
    q_scale = float(HG_DIM ** -0.5)
    nt_dims = (((1,), (1,)), ((), ()))
    tn_dims = (((0,), (0,)), ((), ()))

    def lower_bound(direction):
        a0 = lbr[direction:direction + 1, :]
        a1 = lbr[2 + direction:3 + direction, :]
        mx = jnp.maximum(a0, a1)
        e0 = jnp.exp(a0 - mx)
        e1 = jnp.exp(a1 - mx)
        return e0 / (e0 + e1)

    lbs = (lower_bound(0), lower_bound(1))

    def chunk_order(direction, nchunks):
        return range(nchunks) if direction == 0 else range(nchunks - 1, -1, -1)

    def gates(direction, r0, nchunks):
        n = nchunks * c
        f_ref = ff_ref if direction == 0 else fb_ref
        lb = lbs[direction]
        f = lb + (1.0 - lb) * jax.nn.sigmoid(f_ref[pl.ds(r0, n), :])
        kk = 1.0 - f
        lf = jnp.log(f) * LOG2E
        pos = lax.broadcasted_iota(jnp.int32, (n, HG_DIM), 0) & (c - 1)
        bcum = lf
        for sh in (1, 2, 4, 8, 16, 32):
            if direction == 0:
                bcum = bcum + jnp.where(pos >= sh, pltpu.roll(bcum, sh, axis=0), 0.0)
            else:
                bcum = bcum + jnp.where(pos < c - sh, pltpu.roll(bcum, n - sh, axis=0), 0.0)
        last = c - 1 if direction == 0 else 0
        b_last = [bcum[j * c + last:j * c + last + 1, :] for j in range(nchunks)]
        b_last_rows = jnp.concatenate([jnp.broadcast_to(bl, (c, HG_DIM)) for bl in b_last], axis=0)
        k_tail = (kk * jnp.exp2(b_last_rows - bcum)).astype(BF16)
        return kk, bcum, k_tail, jnp.exp2(jnp.concatenate(b_last, axis=0))

    def state_update(st, v, k_tail, decay_row):
        u_t = lax.dot_general(v, k_tail, tn_dims, preferred_element_type=F32)
        return st * decay_row + u_t

    for direction in (0, 1):
        nchunks = ctx_len // c
        _, _, k_tail, decay = gates(direction, length, nchunks)
        st = jnp.zeros((HG_DIM, HG_DIM), F32)
        for j in chunk_order(direction, nchunks):
            sl = slice(j * c, (j + 1) * c)
            st = state_update(st, hi_ref[length + j * c:length + (j + 1) * c, :], k_tail[sl], decay[j:j + 1, :])
        st_ref[direction] = st

    def groups_of(s):
        return (s, n_groups - 1 - s)

    pc = 2
    pr = pc * c
    units = HG_GROUP // pc

    def prepare_unit(s, slot, direction, u):
        g = groups_of(s)[direction]
        r0 = pl.multiple_of(g * gsz + u * pr, pr)
        rows = slice(u * pr, (u + 1) * pr)
        kk, bcum, k_tail, decay = gates(direction, r0, pc)
        q = _silu(hq_ref[pl.ds(r0, pr), :].astype(F32)) * q_scale
        qd_ref[slot, direction, rows, :] = (q * jnp.exp2(bcum)).astype(BF16)
        kd_ref[slot, direction, rows, :] = (kk * jnp.exp2(-bcum)).astype(BF16)
        kt_ref[slot, direction, rows, :] = k_tail
        dec_ref[slot, direction, u * pc:(u + 1) * pc, :] = decay

    def consume_unit(s, slot, direction, u, st, finalize):
        g = groups_of(s)[direction]
        o0 = pl.multiple_of(g * gsz + u * pr, pr)
        outs = [None] * pc
        for jj in chunk_order(direction, pc):
            j = u * pc + jj
            sl = slice(j * c, (j + 1) * c)
            q_d = qd_ref[slot, direction, sl, :]
            v = hi_ref[pl.ds(o0 + jj * c, c), :]
            sc = lax.dot_general(q_d, kd_ref[slot, direction, sl, :], nt_dims, preferred_element_type=F32)
            sc = jnp.where(tri_c[direction], sc, 0.0).astype(BF16)
            outs[jj] = (jnp.dot(sc, v, preferred_element_type=F32)
                        + lax.dot_general(q_d, st.astype(BF16), nt_dims, preferred_element_type=F32))
            st = state_update(st, v, kt_ref[slot, direction, sl, :], dec_ref[slot, direction, j:j + 1, :])
        out = jnp.concatenate(outs, axis=0)
        if finalize:
            tot = oacc_ref[pl.ds(o0, pr), :] + out
            gate = _silu(hg_ref[pl.ds(o0, pr), :].astype(F32))
            o_ref[pl.ds(o0, pr), :] = (_rms(tot) * gn_ref[...] * gate).astype(o_ref.dtype)
        else:
            oacc_ref[pl.ds(o0, pr), :] = out
        return st

    def run(cons, prep):
        st = [st_ref[0], st_ref[1]] if cons is not None else None
        for idx in range(units):
            for direction in (0, 1):
                u = chunk_order(direction, units)[idx]
                if cons is not None:
                    st[direction] = consume_unit(cons[0], cons[1], direction, u, st[direction], cons[2])
                if prep is not None:
                    prepare_unit(prep[0], prep[1], direction, u)
        if cons is not None:
            st_ref[0] = st[0]
            st_ref[1] = st[1]

    def pair(finalize):
        def body(i, carry):
            run((2 * i, 0, finalize), (2 * i + 1, 1))
            run((2 * i + 1, 1, finalize), (2 * i + 2, 0))
            return carry
        return body

    run(None, (0, 0))
    lax.fori_loop(0, half // 2, pair(False), 0)
    lax.fori_loop(half // 2, n_groups // 2 - 1, pair(True), 0)
    run((n_groups - 2, 0, True), (n_groups - 1, 1))
    run((n_groups - 1, 1, True), None)


def _hgrn(p16, p32, lb_rows, gn, ctx_len):
    bsz, lk, _ = p16.shape
    length = lk - ctx_len
    gsz = HG_GROUP * HG_CHUNK
    assert (length // gsz) % 4 == 0 and ctx_len <= gsz
    strip = lambda col: pl.BlockSpec((None, lk, LANES), lambda b, h: (b, 0, col // LANES + h))
    whole = lambda arr: pl.BlockSpec(arr.shape, lambda b, h: (0,) * arr.ndim)
    return pl.pallas_call(
        functools.partial(_hgrn_kernel, ctx_len=ctx_len, length=length),
        grid=(bsz, HEADS),
        in_specs=[strip(COL_HQ), strip(COL_HI), strip(COL_HG), strip(0), strip(HEADS * HG_DIM),
                  pl.BlockSpec((None, 4, LANES), lambda b, h: (h, 0, 0)),
                  whole(gn)],
        out_specs=pl.BlockSpec((None, length, LANES), lambda b, h: (b, 0, h)),
        out_shape=jax.ShapeDtypeStruct((bsz, length, HEADS * HG_DIM), BF16),
        scratch_shapes=[pltpu.VMEM((length, HG_DIM), F32), pltpu.VMEM((2, HG_DIM, HG_DIM), F32),
                        pltpu.VMEM((2, 2, gsz, HG_DIM), BF16), pltpu.VMEM((2, 2, gsz, HG_DIM), BF16),
                        pltpu.VMEM((2, 2, gsz, HG_DIM), BF16), pltpu.VMEM((2, 2, HG_GROUP, HG_DIM), F32)],
        compiler_params=_params(("arbitrary", "arbitrary"), 48),
        name="hgrn",
    )(p16, p16, p16, p32, p32, lb_rows, gn)


def _merge_kernel(attn_ref, hg_ref, gl_ref, x_ref, mod_ref, ng_ref, wa_ref, wh_ref, wo_ref, x1_ref, h2_ref, *, d):
    b = pl.program_id(0)
    gt_a = mod_ref[pl.ds(b, 1), 2 * d:3 * d]
    sh_f = mod_ref[pl.ds(b, 1), 3 * d:4 * d]
    sc_f = mod_ref[pl.ds(b, 1), 4 * d:5 * d]
    gain_a = ng_ref[1:2, :] * gt_a
    gain_f = ng_ref[2:3, :] * (1.0 + sc_f)
    rows_half = attn_ref.shape[0] // 2
    for r0 in (0, rows_half):
        rs = slice(r0, r0 + rows_half)
        ya = jnp.dot(attn_ref[rs, :], wa_ref[...], preferred_element_type=F32)
        yh = jnp.dot(hg_ref[rs, :], wh_ref[...], preferred_element_type=F32)
        g0 = jax.nn.sigmoid(gl_ref[rs, 0:d].astype(F32))
        g1 = jax.nn.sigmoid(gl_ref[rs, d:2 * d].astype(F32))
        z = (g0 * ya + g1 * yh).astype(BF16)
        o = jnp.dot(z, wo_ref[...], preferred_element_type=F32)
        x1 = x_ref[rs, :] + _rms(o) * gain_a
        x1_ref[rs, :] = x1
        h2_ref[rs, :] = (_rms(x1) * gain_f + sh_f).astype(BF16)


def _merge(attn, hgn, p16, x, mod, norm_g, w_bm, w_bh, w_o):
    bsz, length, d = x.shape
    tm = _pick(length, (512, 256))
    whole = lambda arr: pl.BlockSpec(arr.shape, lambda b, i: (0,) * arr.ndim, pipeline_mode=pl.Buffered(1))
    return pl.pallas_call(
        functools.partial(_merge_kernel, d=d),
        grid=(bsz, length // tm),
        in_specs=[pl.BlockSpec((None, tm, attn.shape[2]), lambda b, i: (b, i, 0)),
                  pl.BlockSpec((None, tm, hgn.shape[2]), lambda b, i: (b, i, 0)),
                  pl.BlockSpec((None, tm, 2 * d), lambda b, i: (b, i, COL_GATES // (2 * d))),
                  pl.BlockSpec((None, tm, d), lambda b, i: (b, i, 0)),
                  whole(mod), whole(norm_g), whole(w_bm), whole(w_bh), whole(w_o)],
        out_specs=[pl.BlockSpec((None, tm, d), lambda b, i: (b, i, 0)),
                   pl.BlockSpec((None, tm, d), lambda b, i: (b, i, 0))],
        out_shape=[jax.ShapeDtypeStruct((bsz, length, d), F32),
                   jax.ShapeDtypeStruct((bsz, length, d), BF16)],
        compiler_params=_params(("arbitrary", "arbitrary"), 56),
        name="merge",
    )(attn, hgn, p16, x, mod, norm_g, w_bm, w_bh, w_o)


def _ffn_kernel(h_ref, wa_ref, wb_ref, wo_ref, x1_ref, mod_ref, ng_ref, o_ref, acc_ref, *, d, nf):
    b = pl.program_id(0)
    f = pl.program_id(2)

    @pl.when(f == 0)
    def _():
        acc_ref[...] = jnp.zeros(acc_ref.shape, F32)

    h = h_ref[...]
    a = jnp.dot(h, wa_ref[...], preferred_element_type=F32)
    g = jnp.dot(h, wb_ref[...], preferred_element_type=F32)
    act = (_silu(a) * g).astype(BF16)
    acc_ref[...] += jnp.dot(act, wo_ref[...], preferred_element_type=F32)

    @pl.when(f == nf - 1)
    def _():
        gt_f = mod_ref[pl.ds(b, 1), 5 * d:6 * d]
        o_ref[...] = x1_ref[...] + _rms(acc_ref[...]) * (ng_ref[3:4, :] * gt_f)


def _ffn(h2, x1, mod, norm_g, w_fi, w_fo):
    bsz, length, d = x1.shape
    d_ff = w_fo.shape[0]
    tm = _pick(length, (512, 256))
    tf = _pick(d_ff, (512, 256, 128))
    nf = d_ff // tf
    whole = lambda arr: pl.BlockSpec(arr.shape, lambda b, i, f: (0,) * arr.ndim)
    return pl.pallas_call(
        functools.partial(_ffn_kernel, d=d, nf=nf),
        grid=(bsz, length // tm, nf),
        in_specs=[pl.BlockSpec((None, tm, d), lambda b, i, f: (b, i, 0)),
                  pl.BlockSpec((d, tf), lambda b, i, f: (0, f)),
                  pl.BlockSpec((d, tf), lambda b, i, f: (0, nf + f)),
                  pl.BlockSpec((tf, d), lambda b, i, f: (f, 0)),
                  pl.BlockSpec((None, tm, d), lambda b, i, f: (b, i, 0)),
                  whole(mod), whole(norm_g)],
        out_specs=pl.BlockSpec((None, tm, d), lambda b, i, f: (b, i, 0)),
        out_shape=jax.ShapeDtypeStruct((bsz, length, d), F32),
        scratch_shapes=[pltpu.VMEM((tm, d), F32)],
        compiler_params=_params(("arbitrary", "arbitrary", "arbitrary"), 48),
        name="ffn",
    )(h2, w_fi, w_fi, w_fo, x1, mod, norm_g)


def _rope_tables(length, ctx_len):
    rows = length // GRID_W
    pairs = QK_ROPE // 4
    row = np.repeat(np.arange(rows, dtype=np.float64), GRID_W)
    col = np.tile(np.arange(GRID_W, dtype=np.float64), rows)
    inv = ROPE_THETA ** (-np.arange(pairs, dtype=np.float64) / pairs)
    ang = np.concatenate([row[:, None] * inv, col[:, None] * inv], axis=-1)
    rep = LANES // (QK_ROPE // 2)
    cos = np.concatenate([np.tile(np.cos(ang), (1, rep)), np.ones((ctx_len, LANES))], axis=0).astype(np.float32)
    sin = np.concatenate([np.tile(np.sin(ang), (1, rep)), np.zeros((ctx_len, LANES))], axis=0).astype(np.float32)
    sign = np.where((np.arange(LANES) % QK_ROPE) < QK_ROPE // 2, -1.0, 1.0).astype(np.float32)
    sin_signed_t = np.ascontiguousarray(sin.T * sign[:, None])
    return tuple(jnp.asarray(t) for t in (cos, sin, np.ascontiguousarray(cos.T), sin_signed_t))


def kernel(x, c, ctx, c_ctx, w_mod, b_mod, norm_g, w_in, mla_q_norm, mla_kv_norm, w_uq, w_ukv,
           hgrn_lb, hgrn_o_norm, w_br_mla, w_br_hgrn, w_out, w_ffn_in, w_ffn_out):
    bsz, length, d = x.shape
    ctx_len = ctx.shape[1]
    depth = w_mod.shape[0]
    assert depth == 1 and hgrn_lb.shape[0] == 2, "kernel is written for the depth-1 block"
    assert bsz < 8 and length % (HG_GROUP * HG_CHUNK) == 0 and ctx_len % HG_CHUNK == 0
    lk = length + ctx_len

    w16, w32 = _w_in_prep(jnp.swapaxes(w_in, 1, 2))

    wq = w_uq[0].reshape(Q_LORA, HEADS, QK_NOPE + QK_ROPE)
    wq_nope, wq_rope = wq[..., :QK_NOPE], wq[..., QK_NOPE:]
    zpad = jnp.zeros((Q_LORA, HEADS, QK_PAD - QK_NOPE - QK_ROPE), F32)
    wqm_t = jnp.concatenate([wq_nope, wq_rope, zpad], axis=-1).reshape(Q_LORA, HEADS * QK_PAD).T.astype(BF16)
    wkv = w_ukv[0].reshape(KV_LORA, HEADS, QK_NOPE + V_HEAD)
    wk = wkv[..., :QK_NOPE].reshape(KV_LORA, HEADS * QK_NOPE).astype(BF16)
    wvt = wkv[..., QK_NOPE:].reshape(KV_LORA, HEADS * V_HEAD).T.astype(BF16)
    w_bm = w_br_mla[0].astype(BF16)
    w_bh = w_br_hgrn[0].astype(BF16)
    w_o = w_out[0].astype(BF16)
    w_fi = w_ffn_in[0].astype(BF16)
    w_fo = w_ffn_out[0].astype(BF16)
    lb_rows = hgrn_lb.transpose(2, 0, 1, 3).reshape(HEADS, 4, HG_DIM)
    tables = _rope_tables(length, ctx_len)

    cc = jnp.concatenate([c, c_ctx[None, :], jnp.zeros((8 - bsz - 1, d), F32)], axis=0)
    mod = _mod(cc, w_mod[0], b_mod[0][None, :])
    ng = norm_g[0]

    hcat = _norm_mod(x, ctx, mod, ng)
    h2d = hcat.reshape(bsz * lk, d)
    p16 = _matmul(h2d, w16, BF16, "in_proj16").reshape(bsz, lk, N16)
    p32 = _matmul(h2d, w32, F32, "in_proj32").reshape(bsz, lk, 2 * HEADS * HG_DIM)
    qt, k, vt = _mla_prep(p16, tables, mla_q_norm[0][None, :], mla_kv_norm[0][None, :],
                          wqm_t, wk, wvt, ctx_len)
    attn = _attention(qt, k, vt)
    hgn = _hgrn(p16, p32, lb_rows, hgrn_o_norm[0][None, :], ctx_len)
    x1, h2 = _merge(attn, hgn, p16, x, mod, ng, w_bm, w_bh, w_o)
    return _ffn(h2, x1, mod, ng, w_fi, w_fo)
```

```python
import functools

import jax
import jax.numpy as jnp
import numpy as np
from jax import lax
from jax.experimental import pallas as pl
from jax.experimental.pallas import tpu as pltpu

F32 = jnp.float32
BF16 = jnp.bfloat16

RMS_EPS = 1e-6
N_MOD = 6
GRID_W = 64
ROPE_THETA = 10000.0
HEADS = 8
Q_LORA = 512
KV_LORA = 256
QK_NOPE = 128
QK_ROPE = 64
V_HEAD = 128
QK_PAD = 256
HG_DIM = 128
HG_CHUNK = 64
HG_GROUP = 8
LANES = 128
NORM_SLAB = 16
KV_TILE = 256
VT_ROWS = 144
LOG2E = 1.4426950408889634
MIB = 1024 * 1024

COL_GATES = 0
COL_HQ = 4096
COL_HI = 5120
COL_HG = 6144
COL_CQ = 7168
COL_CKV = 7680
COL_KR = 7936
N16 = 8192


def _pick(n, candidates):
    for c in candidates:
        if n % c == 0:
            return c
    raise ValueError(f"no tile for {n} in {candidates}")


def _params(sem, vmem_mib):
    return pltpu.CompilerParams(dimension_semantics=sem, vmem_limit_bytes=vmem_mib * MIB)


def _rms(xf):
    return xf * lax.rsqrt(jnp.mean(xf * xf, axis=-1, keepdims=True) + RMS_EPS)


def _silu(x):
    return x * jax.nn.sigmoid(x)


def _mod_kernel(c_ref, w_ref, b_ref, o_ref):
    s = _silu(c_ref[...]).astype(BF16)
    o_ref[...] = jnp.dot(s, w_ref[...].astype(BF16), preferred_element_type=F32) + b_ref[...]


def _mod(cc, w_mod, b_mod):
    d, n = w_mod.shape
    tn = _pick(n, (1024, 512, 256, 128))
    return pl.pallas_call(
        _mod_kernel,
        grid=(n // tn,),
        in_specs=[pl.BlockSpec((8, d), lambda j: (0, 0)),
                  pl.BlockSpec((d, tn), lambda j: (0, j)),
                  pl.BlockSpec((1, tn), lambda j: (0, j))],
        out_specs=pl.BlockSpec((8, tn), lambda j: (0, j)),
        out_shape=jax.ShapeDtypeStruct((8, n), F32),
        compiler_params=_params(("arbitrary",), 40),
        name="mod",
    )(cc, w_mod, b_mod)


def _norm_mod_kernel(x_ref, ctx_ref, mod_ref, g_ref, o_ref, *, d, ctx_row, nt):
    b = pl.program_id(0)
    t = pl.program_id(1)
    g = g_ref[0:1, :]

    def emit(src_ref, row):
        sh = mod_ref[pl.ds(row, 1), 0:d]
        sc1 = 1.0 + mod_ref[pl.ds(row, 1), d:2 * d]

        def slab(r, carry):
            rows = pl.ds(pl.multiple_of(r * NORM_SLAB, NORM_SLAB), NORM_SLAB)
            o_ref[rows, :] = (_rms(src_ref[rows, :]) * g * sc1 + sh).astype(BF16)
            return carry

        lax.fori_loop(0, src_ref.shape[0] // NORM_SLAB, slab, 0, unroll=4)

    @pl.when(t < nt)
    def _():
        emit(x_ref, b)

    @pl.when(t == nt)
    def _():
        emit(ctx_ref, ctx_row)


def _norm_mod(x, ctx, mod, norm_g):
    bsz, length, d = x.shape
    ctx_len = ctx.shape[1]
    tm = _pick(length, (1024, 512, 256))
    nt = length // tm
    assert ctx_len <= tm
    return pl.pallas_call(
        functools.partial(_norm_mod_kernel, d=d, ctx_row=bsz, nt=nt),
        grid=(bsz, nt + 1),
        in_specs=[pl.BlockSpec((None, tm, d), lambda b, t: (b, jnp.minimum(t, nt - 1), 0)),
                  pl.BlockSpec((None, ctx_len, d), lambda b, t: (b, 0, 0)),
                  pl.BlockSpec(mod.shape, lambda b, t: (0, 0)),
                  pl.BlockSpec(norm_g.shape, lambda b, t: (0, 0))],
        out_specs=pl.BlockSpec((None, tm, d), lambda b, t: (b, t, 0)),
        out_shape=jax.ShapeDtypeStruct((bsz, length + ctx_len, d), BF16),
        compiler_params=_params(("arbitrary", "arbitrary"), 40),
        name="norm_mod",
    )(x, ctx, mod, norm_g)


def _w_in_kernel(wt_ref, o16_ref, o32_ref, *, d):
    o = [0]
    for width in (Q_LORA, KV_LORA, QK_ROPE, 1024, 1024, 1024, 1024, 1024, 2 * d):
        o.append(o[-1] + width)

    def grab(lo, hi):
        return jnp.transpose(wt_ref[lo:hi, :]).astype(BF16)

    step = 1024
    for c in range(0, 2 * d, step):
        o16_ref[:, COL_GATES + c:COL_GATES + c + step] = grab(o[8] + c, o[8] + c + step)
    o16_ref[:, COL_HQ:COL_HQ + 1024] = grab(o[3], o[4])
    o16_ref[:, COL_HI:COL_HI + 1024] = grab(o[6], o[7])
    o16_ref[:, COL_HG:COL_HG + 1024] = grab(o[7], o[8])
    o16_ref[:, COL_CQ:COL_KR] = grab(o[0], o[2])
    kr = grab(o[2], o[2] + LANES)[:, 0:QK_ROPE]
    half = QK_ROPE // 2
    tail = jnp.zeros((kr.shape[0], N16 - COL_KR - 2 * QK_ROPE), BF16)
    o16_ref[:, COL_KR:N16] = jnp.concatenate([kr, -kr[:, half:], kr[:, :half], tail], axis=1)
    o32_ref[:, 0:1024] = grab(o[4], o[5])
    o32_ref[:, 1024:2048] = grab(o[5], o[6])


def _w_in_prep(w3d_t):
    _, n, k = w3d_t.shape
    tk = _pick(k, (256, 128))
    return pl.pallas_call(
        functools.partial(_w_in_kernel, d=k),
        grid=(k // tk,),
        in_specs=[pl.BlockSpec((None, n, tk), lambda i: (0, 0, i))],
        out_specs=[pl.BlockSpec((tk, N16), lambda i: (i, 0)),
                   pl.BlockSpec((tk, 2 * HEADS * HG_DIM), lambda i: (i, 0))],
        out_shape=[jax.ShapeDtypeStruct((k, N16), BF16), jax.ShapeDtypeStruct((k, 2 * HEADS * HG_DIM), BF16)],
        compiler_params=_params(("arbitrary",), 48),
        name="w_in_prep",
    )(w3d_t)


def _matmul_kernel(a_ref, w_ref, o_ref):
    o_ref[...] = jnp.dot(a_ref[...], w_ref[...], preferred_element_type=F32).astype(o_ref.dtype)


def _matmul(a, w, out_dtype, name):
    m, k = a.shape
    n = w.shape[1]
    tm = _pick(m, (1024, 512, 256))
    tn = _pick(n, (2048, 1024, 512, 256))
    return pl.pallas_call(
        _matmul_kernel,
        grid=(m // tm, n // tn),
        in_specs=[pl.BlockSpec((tm, k), lambda i, j: (i, 0)),
                  pl.BlockSpec((k, tn), lambda i, j: (0, j))],
        out_specs=pl.BlockSpec((tm, tn), lambda i, j: (i, j)),
        out_shape=jax.ShapeDtypeStruct((m, n), out_dtype),
        compiler_params=_params(("arbitrary", "arbitrary"), 48),
        name=name,
    )(a, w)


def _mla_prep_kernel(cq_ref, ckv_ref, kr_ref, cos_ref, sin_ref, cost_ref, sint_ref, gq_ref, gkv_ref,
                     wqmt_ref, wk_ref, wvt_ref, qt_ref, k_ref, vt_ref, *, q_scale, n_lat):
    t = pl.program_id(1)
    nt_dims = (((1,), (1,)), ((), ()))
    half = QK_ROPE // 2

    @pl.when(t < n_lat)
    def _():
        cqn = (_rms(cq_ref[...].astype(F32)) * gq_ref[...]).astype(BF16)
        main_t = lax.dot_general(wqmt_ref[...], cqn, nt_dims, preferred_element_type=F32)
        cos_t = cost_ref[...]
        sin_t = sint_ref[...]
        for h in range(HEADS):
            r0 = h * QK_PAD
            qt_ref[h, 0:QK_NOPE, :] = (main_t[r0:r0 + QK_NOPE, :] * q_scale).astype(BF16)
            rope = main_t[r0 + QK_NOPE:r0 + QK_PAD, :]
            swapped = jnp.concatenate([rope[half:2 * half], rope[0:half], rope[2 * half:]], axis=0)
            qt_ref[h, QK_NOPE:QK_PAD, :] = ((rope * cos_t + swapped * sin_t) * q_scale).astype(BF16)

    cos = cos_ref[...]
    sin = sin_ref[...]
    ckvn = (_rms(ckv_ref[...].astype(F32)) * gkv_ref[...]).astype(BF16)
    kn = jnp.dot(ckvn, wk_ref[...], preferred_element_type=F32)
    vt = lax.dot_general(wvt_ref[...], ckvn, nt_dims, preferred_element_type=F32)
    tm = vt.shape[1]
    ones_rows = (lax.broadcasted_iota(jnp.int32, (VT_ROWS - V_HEAD, tm), 0) == 0).astype(BF16)
    lane = lax.broadcasted_iota(jnp.int32, cos.shape, 1)
    a = kr_ref[...].astype(F32) * jnp.where(lane < QK_ROPE, cos, sin)
    k_rope = (a + pltpu.roll(a, QK_ROPE, axis=1)).astype(BF16)
    for h in range(HEADS):
        c0 = h * QK_PAD
        k_ref[:, c0:c0 + QK_NOPE] = kn[:, h * QK_NOPE:(h + 1) * QK_NOPE].astype(BF16)
        k_ref[:, c0 + QK_NOPE:c0 + QK_PAD] = k_rope
        vt_ref[h, 0:V_HEAD, :] = vt[h * V_HEAD:(h + 1) * V_HEAD, :].astype(BF16)
        vt_ref[h, V_HEAD:VT_ROWS, :] = ones_rows


def _mla_prep(p16, tables, gq, gkv, wqm_t, wk, wvt, ctx_len):
    bsz, lk, _ = p16.shape
    tm = KV_TILE
    assert ctx_len == tm
    nt = lk // tm
    n_lat = nt - 1
    cos_t, sin_t, cos_tt, sin_tt = tables
    whole = lambda arr: pl.BlockSpec(arr.shape, lambda b, t: (0,) * arr.ndim)
    return pl.pallas_call(
        functools.partial(_mla_prep_kernel, q_scale=float((QK_NOPE + QK_ROPE) ** -0.5 * LOG2E), n_lat=n_lat),
        grid=(bsz, nt),
        in_specs=[pl.BlockSpec((None, tm, Q_LORA), lambda b, t: (b, t, COL_CQ // Q_LORA)),
                  pl.BlockSpec((None, tm, KV_LORA), lambda b, t: (b, t, COL_CKV // KV_LORA)),
                  pl.BlockSpec((None, tm, LANES), lambda b, t: (b, t, COL_KR // LANES)),
                  pl.BlockSpec((tm, LANES), lambda b, t: (t, 0)),
                  pl.BlockSpec((tm, LANES), lambda b, t: (t, 0)),
                  pl.BlockSpec((LANES, tm), lambda b, t: (0, t)),
                  pl.BlockSpec((LANES, tm), lambda b, t: (0, t)),
                  whole(gq), whole(gkv), whole(wqm_t), whole(wk), whole(wvt)],
        out_specs=[pl.BlockSpec((None, HEADS, None, QK_PAD, tm), lambda b, t: (b, 0, jnp.minimum(t, n_lat - 1), 0, 0)),
                   pl.BlockSpec((None, tm, HEADS * QK_PAD), lambda b, t: (b, t, 0)),
                   pl.BlockSpec((None, HEADS, None, VT_ROWS, tm), lambda b, t: (b, 0, t, 0, 0))],
        out_shape=[jax.ShapeDtypeStruct((bsz, HEADS, n_lat, QK_PAD, tm), BF16),
                   jax.ShapeDtypeStruct((bsz, lk, HEADS * QK_PAD), BF16),
                   jax.ShapeDtypeStruct((bsz, HEADS, nt, VT_ROWS, tm), BF16)],
        compiler_params=_params(("arbitrary", "arbitrary"), 40),
        name="mla_prep",
    )(p16, p16, p16, cos_t, sin_t, cos_tt, sin_tt, gq, gkv, wqm_t, wk, wvt)


def _attn_kernel(qt_ref, k_ref, vt_ref, o_ref, s_ref, acc_ref, *, tq, kt, nk):
    tk = kt * KV_TILE
    sub = tq // KV_TILE
    nq = qt_ref.shape[0] // sub
    q_cols = [slice(h * KV_TILE, (h + 1) * KV_TILE) for h in range(sub)]
    acc_ref[...] = jnp.zeros(acc_ref.shape, F32)

    def scores(i, j, slot):
        r0 = pl.multiple_of(j * tk, tk)
        ks = k_ref[pl.ds(r0, tk), :]
        cmax = []
        for h, qs in enumerate(q_cols):
            s = jnp.dot(ks, qt_ref[i * sub + h], preferred_element_type=F32)
            s_ref[slot, :, qs] = s
            cmax.append(jnp.max(s, axis=0, keepdims=True))
        return jnp.concatenate(cmax, axis=1)

    def accumulate(j, slot, m_old, cmax):
        m_new = jnp.maximum(m_old, cmax)
        alpha = jnp.exp2(m_old - m_new)
        for qs in q_cols:
            pv = None
            for c in range(kt):
                p = jnp.exp2(s_ref[slot, c * KV_TILE:(c + 1) * KV_TILE, qs] - m_new[:, qs]).astype(BF16)
                d = jnp.dot(vt_ref[j * kt + c], p, preferred_element_type=F32)
                pv = d if pv is None else pv + d
            acc_ref[:, qs] = alpha[:, qs] * acc_ref[:, qs] + pv
        return m_new

    neg_inf = jnp.full((1, tq), -jnp.inf, F32)

    def tile(i, cm0):
        def triple(t, carry):
            m, cm0 = carry
            cm1 = scores(i, 3 * t + 1, 1)
            m = accumulate(3 * t, 0, m, cm0)
            cm2 = scores(i, 3 * t + 2, 2)
            m = accumulate(3 * t + 1, 1, m, cm1)
            cm0 = scores(i, 3 * t + 3, 0)
            m = accumulate(3 * t + 2, 2, m, cm2)
            return m, cm0

        m, cm0 = lax.fori_loop(0, (nk - 2) // 3, triple, (neg_inf, cm0), unroll=True)
        cm1 = scores(i, nk - 1, 1)
        m = accumulate(nk - 2, 0, m, cm0)
        cm_next = scores(jnp.minimum(i + 1, nq - 1), 0, 0)
        accumulate(nk - 1, 1, m, cm1)
        acc = acc_ref[...]
        o = acc[0:V_HEAD, :] / acc[V_HEAD:V_HEAD + 1, :]
        o_ref[pl.ds(pl.multiple_of(i * tq, tq), tq), :] = jnp.transpose(o.astype(o_ref.dtype))
        return cm_next

    lax.fori_loop(0, nq, tile, scores(0, 0, 0))


def _attention(qt, k, vt):
    bsz, heads, n_lat, _, _ = qt.shape
    length = n_lat * KV_TILE
    lk = k.shape[1]
    nc = vt.shape[2]
    tq = _pick(length, (1024,))
    kt = _pick(nc, (3, 1))
    assert (nc // kt) % 3 == 2, "the three-slot chunk pipeline peels two trailing chunks per query tile"
    return pl.pallas_call(
        functools.partial(_attn_kernel, tq=tq, kt=kt, nk=nc // kt),
        grid=(bsz, heads),
        in_specs=[pl.BlockSpec((None, None, n_lat, QK_PAD, KV_TILE), lambda b, h: (b, h, 0, 0, 0)),
                  pl.BlockSpec((None, lk, QK_PAD), lambda b, h: (b, 0, h)),
                  pl.BlockSpec((None, None, nc, VT_ROWS, KV_TILE), lambda b, h: (b, h, 0, 0, 0))],
        out_specs=pl.BlockSpec((None, length, V_HEAD), lambda b, h: (b, 0, h)),
        out_shape=jax.ShapeDtypeStruct((bsz, length, heads * V_HEAD), BF16),
        scratch_shapes=[pltpu.VMEM((3, kt * KV_TILE, tq), F32), pltpu.VMEM((VT_ROWS, tq), F32)],
        compiler_params=_params(("arbitrary", "arbitrary"), 48),
        name="attn",
    )(qt, k, vt)


def _hgrn_kernel(hq_ref, hi_ref, hg_ref, ff_ref, fb_ref, lb_ref, gn_ref, o_ref,
                 oacc_ref, st_ref, qd_ref, kd_ref, kt_ref, dec_ref, *, ctx_len, length):
    c = HG_CHUNK
    gsz = HG_GROUP * c
    n_groups = length // gsz
    half = n_groups // 2
    row = lax.broadcasted_iota(jnp.int32, (c, c), 0)
    col = lax.broadcasted_iota(jnp.int32, (c, c), 1)
    tri_c = (row >= col, row <= col)
    lbr = lb_ref[...]
    q_scale = float(HG_DIM ** -0.5)
    nt_dims = (((1,), (1,)), ((), ()))
    tn_dims = (((0,), (0,)), ((), ()))

    def lower_bound(direction):
        a0 = lbr[direction:direction + 1, :]
        a1 = lbr[2 + direction:3 + direction, :]
        mx = jnp.maximum(a0, a1)
        e0 = jnp.exp(a0 - mx)
        e1 = jnp.exp(a1 - mx)
        return e0 / (e0 + e1)

    lbs = (lower_bound(0), lower_bound(1))

    def chunk_order(direction, nchunks):
        return range(nchunks) if direction == 0 else range(nchunks - 1, -1, -1)

    def gates(direction, r0, nchunks):
        n = nchunks * c
        f_ref = ff_ref if direction == 0 else fb_ref
        lb = lbs[direction]
        f = lb + (1.0 - lb) * jax.nn.sigmoid(f_ref[pl.ds(r0, n), :])
        kk = 1.0 - f
        lf = jnp.log(f) * LOG2E
        pos = lax.broadcasted_iota(jnp.int32, (n, HG_DIM), 0) & (c - 1)
        bcum = lf
        for sh in (1, 2, 4, 8, 16, 32):
            if direction == 0:
                bcum = bcum + jnp.where(pos >= sh, pltpu.roll(bcum, sh, axis=0), 0.0)
            else:
                bcum = bcum + jnp.where(pos < c - sh, pltpu.roll(bcum, n - sh, axis=0), 0.0)
        last = c - 1 if direction == 0 else 0
        b_last = [bcum[j * c + last:j * c + last + 1, :] for j in range(nchunks)]
        b_last_rows = jnp.concatenate([jnp.broadcast_to(bl, (c, HG_DIM)) for bl in b_last], axis=0)
        k_tail = (kk * jnp.exp2(b_last_rows - bcum)).astype(BF16)
        return kk, bcum, k_tail, jnp.exp2(jnp.concatenate(b_last, axis=0))

    def state_update(st, v, k_tail, decay_row):
        u_t = lax.dot_general(v, k_tail, tn_dims, preferred_element_type=F32)
        return st * decay_row + u_t

    for direction in (0, 1):
        nchunks = ctx_len // c
        _, _, k_tail, decay = gates(direction, length, nchunks)
        st = jnp.zeros((HG_DIM, HG_DIM), F32)
        for j in chunk_order(direction, nchunks):
            sl = slice(j * c, (j + 1) * c)
            st = state_update(st, hi_ref[length + j * c:length + (j + 1) * c, :], k_tail[sl], decay[j:j + 1, :])
        st_ref[direction] = st

    def groups_of(s):
        return (s, n_groups - 1 - s)

    pc = 2
    pr = pc * c
    units = HG_GROUP // pc

    def prepare_unit(s, slot, direction, u):
        g = groups_of(s)[direction]
        r0 = pl.multiple_of(g * gsz + u * pr, pr)
        rows = slice(u * pr, (u + 1) * pr)
        kk, bcum, k_tail, decay = gates(direction, r0, pc)
        q = _silu(hq_ref[pl.ds(r0, pr), :].astype(F32)) * q_scale
        qd_ref[slot, direction, rows, :] = (q * jnp.exp2(bcum)).astype(BF16)
        kd_ref[slot, direction, rows, :] = (kk * jnp.exp2(-bcum)).astype(BF16)
        kt_ref[slot, direction, rows, :] = k_tail
        dec_ref[slot, direction, u * pc:(u + 1) * pc, :] = decay

    def consume_unit(s, slot, direction, u, st, finalize):
        g = groups_of(s)[direction]
        o0 = pl.multiple_of(g * gsz + u * pr, pr)
        outs = [None] * pc
        for jj in chunk_order(direction, pc):
            j = u * pc + jj
            sl = slice(j * c, (j + 1) * c)
            q_d = qd_ref[slot, direction, sl, :]
            v = hi_ref[pl.ds(o0 + jj * c, c), :]
            sc = lax.dot_general(q_d, kd_ref[slot, direction, sl, :], nt_dims, preferred_element_type=F32)
            sc = jnp.where(tri_c[direction], sc, 0.0).astype(BF16)
            outs[jj] = (jnp.dot(sc, v, preferred_element_type=F32)
                        + lax.dot_general(q_d, st.astype(BF16), nt_dims, preferred_element_type=F32))
            st = state_update(st, v, kt_ref[slot, direction, sl, :], dec_ref[slot, direction, j:j + 1, :])
        out = jnp.concatenate(outs, axis=0)
        if finalize:
            tot = oacc_ref[pl.ds(o0, pr), :] + out
            gate = _silu(hg_ref[pl.ds(o0, pr), :].astype(F32))
            o_ref[pl.ds(o0, pr), :] = (_rms(tot) * gn_ref[...] * gate).astype(o_ref.dtype)
        else:
            oacc_ref[pl.ds(o0, pr), :] = out
        return st

    def run(cons, prep):
        st = [st_ref[0], st_ref[1]] if cons is not None else None
        for idx in range(units):
            for direction in (0, 1):
                u = chunk_order(direction, units)[idx]
                if cons is not None:
                    st[direction] = consume_unit(cons[0], cons[1], direction, u, st[direction], cons[2])
                if prep is not None:
                    prepare_unit(prep[0], prep[1], direction, u)
        if cons is not None:
            st_ref[0] = st[0]
            st_ref[1] = st[1]

    def pair(finalize):
        def body(i, carry):
            run((2 * i, 0, finalize), (2 * i + 1, 1))
            run((2 * i + 1, 1, finalize), (2 * i + 2, 0))
            return carry
        return body

    run(None, (0, 0))
    lax.fori_loop(0, half // 2, pair(False), 0)
    lax.fori_loop(half // 2, n_groups // 2 - 1, pair(True), 0)
    run((n_groups - 2, 0, True), (n_groups - 1, 1))
    run((n_groups - 1, 1, True), None)


def _hgrn(p16, p32, lb_rows, gn, ctx_len):
    bsz, lk, _ = p16.shape
    length = lk - ctx_len
    gsz = HG_GROUP * HG_CHUNK
    assert (length // gsz) % 4 == 0 and ctx_len <= gsz
    strip = lambda col: pl.BlockSpec((None, lk, LANES), lambda b, h: (b, 0, col // LANES + h))
    whole = lambda arr: pl.BlockSpec(arr.shape, lambda b, h: (0,) * arr.ndim)
    return pl.pallas_call(
        functools.partial(_hgrn_kernel, ctx_len=ctx_len, length=length),
        grid=(bsz, HEADS),
        in_specs=[strip(COL_HQ), strip(COL_HI), strip(COL_HG), strip(0), strip(HEADS * HG_DIM),
                  pl.BlockSpec((None, 4, LANES), lambda b, h: (h, 0, 0)),
                  whole(gn)],
        out_specs=pl.BlockSpec((None, length, LANES), lambda b, h: (b, 0, h)),
        out_shape=jax.ShapeDtypeStruct((bsz, length, HEADS * HG_DIM), BF16),
        scratch_shapes=[pltpu.VMEM((length, HG_DIM), F32), pltpu.VMEM((2, HG_DIM, HG_DIM), F32),
                        pltpu.VMEM((2, 2, gsz, HG_DIM), BF16), pltpu.VMEM((2, 2, gsz, HG_DIM), BF16),
                        pltpu.VMEM((2, 2, gsz, HG_DIM), BF16), pltpu.VMEM((2, 2, HG_GROUP, HG_DIM), F32)],
        compiler_params=_params(("arbitrary", "arbitrary"), 48),
        name="hgrn",
    )(p16, p16, p16, p32, p32, lb_rows, gn)


def _merge_kernel(attn_ref, hg_ref, gl_ref, x_ref, mod_ref, ng_ref, wa_ref, wh_ref, wo_ref, x1_ref, h2_ref, *, d):
    b = pl.program_id(0)
    gt_a = mod_ref[pl.ds(b, 1), 2 * d:3 * d]
    sh_f = mod_ref[pl.ds(b, 1), 3 * d:4 * d]
    sc_f = mod_ref[pl.ds(b, 1), 4 * d:5 * d]
    gain_a = ng_ref[1:2, :] * gt_a
    gain_f = ng_ref[2:3, :] * (1.0 + sc_f)
    rows_half = attn_ref.shape[0] // 2
    for r0 in (0, rows_half):
        rs = slice(r0, r0 + rows_half)
        ya = jnp.dot(attn_ref[rs, :], wa_ref[...], preferred_element_type=F32)
        yh = jnp.dot(hg_ref[rs, :], wh_ref[...], preferred_element_type=F32)
        g0 = jax.nn.sigmoid(gl_ref[rs, 0:d].astype(F32))
        g1 = jax.nn.sigmoid(gl_ref[rs, d:2 * d].astype(F32))
        z = (g0 * ya + g1 * yh).astype(BF16)
        o = jnp.dot(z, wo_ref[...], preferred_element_type=F32)
        x1 = x_ref[rs, :] + _rms(o) * gain_a
        x1_ref[rs, :] = x1
        h2_ref[rs, :] = (_rms(x1) * gain_f + sh_f).astype(BF16)


def _merge(attn, hgn, p16, x, mod, norm_g, w_bm, w_bh, w_o):
    bsz, length, d = x.shape
    tm = _pick(length, (512, 256))
    whole = lambda arr: pl.BlockSpec(arr.shape, lambda b, i: (0,) * arr.ndim, pipeline_mode=pl.Buffered(1))
    return pl.pallas_call(
        functools.partial(_merge_kernel, d=d),
        grid=(bsz, length // tm),
        in_specs=[pl.BlockSpec((None, tm, attn.shape[2]), lambda b, i: (b, i, 0)),
                  pl.BlockSpec((None, tm, hgn.shape[2]), lambda b, i: (b, i, 0)),
                  pl.BlockSpec((None, tm, 2 * d), lambda b, i: (b, i, COL_GATES // (2 * d))),
                  pl.BlockSpec((None, tm, d), lambda b, i: (b, i, 0)),
                  whole(mod), whole(norm_g), whole(w_bm), whole(w_bh), whole(w_o)],
        out_specs=[pl.BlockSpec((None, tm, d), lambda b, i: (b, i, 0)),
                   pl.BlockSpec((None, tm, d), lambda b, i: (b, i, 0))],
        out_shape=[jax.ShapeDtypeStruct((bsz, length, d), F32),
                   jax.ShapeDtypeStruct((bsz, length, d), BF16)],
        compiler_params=_params(("arbitrary", "arbitrary"), 56),
        name="merge",
    )(attn, hgn, p16, x, mod, norm_g, w_bm, w_bh, w_o)


def _ffn_kernel(h_ref, wa_ref, wb_ref, wo_ref, x1_ref, mod_ref, ng_ref, o_ref, acc_ref, *, d, nf):
    b = pl.program_id(0)
    f = pl.program_id(2)

    @pl.when(f == 0)
    def _():
        acc_ref[...] = jnp.zeros(acc_ref.shape, F32)

    h = h_ref[...]
    a = jnp.dot(h, wa_ref[...], preferred_element_type=F32)
    g = jnp.dot(h, wb_ref[...], preferred_element_type=F32)
    act = (_silu(a) * g).astype(BF16)
    acc_ref[...] += jnp.dot(act, wo_ref[...], preferred_element_type=F32)

    @pl.when(f == nf - 1)
    def _():
        gain = ng_ref[3:4, :] * mod_ref[pl.ds(b, 1), 5 * d:6 * d]

        def slab(r, carry):
            rows = pl.ds(pl.multiple_of(r * NORM_SLAB, NORM_SLAB), NORM_SLAB)
            o_ref[rows, :] = x1_ref[rows, :] + _rms(acc_ref[rows, :]) * gain
            return carry

        lax.fori_loop(0, o_ref.shape[0] // NORM_SLAB, slab, 0, unroll=4)


def _ffn(h2, x1, mod, norm_g, w_fi, w_fo):
    bsz, length, d = x1.shape
    d_ff = w_fo.shape[0]
    tm = _pick(length, (512, 256))
    tf = _pick(d_ff, (512, 256, 128))
    nf = d_ff // tf
    whole = lambda arr: pl.BlockSpec(arr.shape, lambda b, i, f: (0,) * arr.ndim)
    return pl.pallas_call(
        functools.partial(_ffn_kernel, d=d, nf=nf),
        grid=(bsz, length // tm, nf),
        in_specs=[pl.BlockSpec((None, tm, d), lambda b, i, f: (b, i, 0)),
                  pl.BlockSpec((d, tf), lambda b, i, f: (0, f)),
                  pl.BlockSpec((d, tf), lambda b, i, f: (0, nf + f)),
                  pl.BlockSpec((tf, d), lambda b, i, f: (f, 0)),
                  pl.BlockSpec((None, tm, d), lambda b, i, f: (b, i, 0)),
                  whole(mod), whole(norm_g)],
        out_specs=pl.BlockSpec((None, tm, d), lambda b, i, f: (b, i, 0)),
        out_shape=jax.ShapeDtypeStruct((bsz, length, d), F32),
        scratch_shapes=[pltpu.VMEM((tm, d), F32)],
        compiler_params=_params(("arbitrary", "arbitrary", "arbitrary"), 48),
        name="ffn",
    )(h2, w_fi, w_fi, w_fo, x1, mod, norm_g)


def _rope_tables(length, ctx_len):
    rows = length // GRID_W
    pairs = QK_ROPE // 4
    row = np.repeat(np.arange(rows, dtype=np.float64), GRID_W)
    col = np.tile(np.arange(GRID_W, dtype=np.float64), rows)
    inv = ROPE_THETA ** (-np.arange(pairs, dtype=np.float64) / pairs)
    ang = np.concatenate([row[:, None] * inv, col[:, None] * inv], axis=-1)
    rep = LANES // (QK_ROPE // 2)
    cos = np.concatenate([np.tile(np.cos(ang), (1, rep)), np.ones((ctx_len, LANES))], axis=0).astype(np.float32)
    sin = np.concatenate([np.tile(np.sin(ang), (1, rep)), np.zeros((ctx_len, LANES))], axis=0).astype(np.float32)
    sign = np.where((np.arange(LANES) % QK_ROPE) < QK_ROPE // 2, -1.0, 1.0).astype(np.float32)
    sin_signed_t = np.ascontiguousarray(sin.T * sign[:, None])
    return tuple(jnp.asarray(t) for t in (cos, sin, np.ascontiguousarray(cos.T), sin_signed_t))


def kernel(x, c, ctx, c_ctx, w_mod, b_mod, norm_g, w_in, mla_q_norm, mla_kv_norm, w_uq, w_ukv,
           hgrn_lb, hgrn_o_norm, w_br_mla, w_br_hgrn, w_out, w_ffn_in, w_ffn_out):
    bsz, length, d = x.shape
    ctx_len = ctx.shape[1]
    depth = w_mod.shape[0]
    assert depth == 1 and hgrn_lb.shape[0] == 2, "kernel is written for the depth-1 block"
    assert bsz < 8 and length % (HG_GROUP * HG_CHUNK) == 0 and ctx_len % HG_CHUNK == 0
    lk = length + ctx_len

    w16, w32 = _w_in_prep(jnp.swapaxes(w_in, 1, 2))

    wq = w_uq[0].reshape(Q_LORA, HEADS, QK_NOPE + QK_ROPE)
    wq_nope, wq_rope = wq[..., :QK_NOPE], wq[..., QK_NOPE:]
    zpad = jnp.zeros((Q_LORA, HEADS, QK_PAD - QK_NOPE - QK_ROPE), F32)
    wqm_t = jnp.concatenate([wq_nope, wq_rope, zpad], axis=-1).reshape(Q_LORA, HEADS * QK_PAD).T.astype(BF16)
    wkv = w_ukv[0].reshape(KV_LORA, HEADS, QK_NOPE + V_HEAD)
    wk = wkv[..., :QK_NOPE].reshape(KV_LORA, HEADS * QK_NOPE).astype(BF16)
    wvt = wkv[..., QK_NOPE:].reshape(KV_LORA, HEADS * V_HEAD).T.astype(BF16)
    w_bm = w_br_mla[0].astype(BF16)
    w_bh = w_br_hgrn[0].astype(BF16)
    w_o = w_out[0].astype(BF16)
    w_fi = w_ffn_in[0].astype(BF16)
    w_fo = w_ffn_out[0].astype(BF16)
    lb_rows = hgrn_lb.transpose(2, 0, 1, 3).reshape(HEADS, 4, HG_DIM)
    tables = _rope_tables(length, ctx_len)

    cc = jnp.concatenate([c, c_ctx[None, :], jnp.zeros((8 - bsz - 1, d), F32)], axis=0)
    mod = _mod(cc, w_mod[0], b_mod[0][None, :])
    ng = norm_g[0]

    hcat = _norm_mod(x, ctx, mod, ng)
    h2d = hcat.reshape(bsz * lk, d)
    p16 = _matmul(h2d, w16, BF16, "in_proj16").reshape(bsz, lk, N16)
    p32 = _matmul(h2d, w32, F32, "in_proj32").reshape(bsz, lk, 2 * HEADS * HG_DIM)
    qt, k, vt = _mla_prep(p16, tables, mla_q_norm[0][None, :], mla_kv_norm[0][None, :],
                          wqm_t, wk, wvt, ctx_len)
    attn = _attention(qt, k, vt)
    hgn = _hgrn(p16, p32, lb_rows, hgrn_o_norm[0][None, :], ctx_len)
    x1, h2 = _merge(attn, hgn, p16, x, mod, ng, w_bm, w_bh, w_o)
    return _ffn(h2, x1, mod, ng, w_fi, w_fo)
```

```python
import functools

import jax
import jax.numpy as jnp
import numpy as np
from jax import lax
from jax.experimental import pallas as pl
from jax.experimental.pallas import tpu as pltpu

F32 = jnp.float32
BF16 = jnp.bfloat16

RMS_EPS = 1e-6
N_MOD = 6
GRID_W = 64
ROPE_THETA = 10000.0
HEADS = 8
Q_LORA = 512
KV_LORA = 256
QK_NOPE = 128
QK_ROPE = 64
V_HEAD = 128
QK_PAD = 256
HG_DIM = 128
HG_CHUNK = 64
HG_GROUP = 8
LANES = 128
FFN_TILE = 512
NORM_SLAB = 16
KV_TILE = 256
VT_ROWS = 144
LOG2E = 1.4426950408889634
MIB = 1024 * 1024

COL_GATES = 0
COL_HQ = 4096
COL_HI = 5120
COL_HG = 6144
COL_CQ = 7168
COL_CKV = 7680
COL_KR = 7936
N16 = 8192


def _pick(n, candidates):
    for c in candidates:
        if n % c == 0:
            return c
    raise ValueError(f"no tile for {n} in {candidates}")


def _params(sem, vmem_mib):
    return pltpu.CompilerParams(dimension_semantics=sem, vmem_limit_bytes=vmem_mib * MIB)


def _rms(xf):
    return xf * lax.rsqrt(jnp.mean(xf * xf, axis=-1, keepdims=True) + RMS_EPS)


def _silu(x):
    return x * jax.nn.sigmoid(x)


def _mod_kernel(c_ref, w_ref, b_ref, o_ref):
    s = _silu(c_ref[...]).astype(BF16)
    o_ref[...] = jnp.dot(s, w_ref[...].astype(BF16), preferred_element_type=F32) + b_ref[...]


def _mod(cc, w_mod, b_mod):
    d, n = w_mod.shape
    tn = _pick(n, (1024, 512, 256, 128))
    return pl.pallas_call(
        _mod_kernel,
        grid=(n // tn,),
        in_specs=[pl.BlockSpec((8, d), lambda j: (0, 0)),
                  pl.BlockSpec((d, tn), lambda j: (0, j)),
                  pl.BlockSpec((1, tn), lambda j: (0, j))],
        out_specs=pl.BlockSpec((8, tn), lambda j: (0, j)),
        out_shape=jax.ShapeDtypeStruct((8, n), F32),
        compiler_params=_params(("arbitrary",), 40),
        name="mod",
    )(cc, w_mod, b_mod)


def _norm_mod_kernel(x_ref, ctx_ref, mod_ref, g_ref, o_ref, *, d, ctx_row, nt):
    b = pl.program_id(0)
    t = pl.program_id(1)
    g = g_ref[0:1, :]

    def emit(src_ref, row):
        sh = mod_ref[pl.ds(row, 1), 0:d]
        sc1 = 1.0 + mod_ref[pl.ds(row, 1), d:2 * d]

        def slab(r, carry):
            rows = pl.ds(pl.multiple_of(r * NORM_SLAB, NORM_SLAB), NORM_SLAB)
            o_ref[rows, :] = (_rms(src_ref[rows, :]) * g * sc1 + sh).astype(BF16)
            return carry

        lax.fori_loop(0, src_ref.shape[0] // NORM_SLAB, slab, 0, unroll=4)

    @pl.when(t < nt)
    def _():
        emit(x_ref, b)

    @pl.when(t == nt)
    def _():
        emit(ctx_ref, ctx_row)


def _norm_mod(x, ctx, mod, norm_g):
    bsz, length, d = x.shape
    ctx_len = ctx.shape[1]
    tm = _pick(length, (1024, 512, 256))
    nt = length // tm
    assert ctx_len <= tm
    return pl.pallas_call(
        functools.partial(_norm_mod_kernel, d=d, ctx_row=bsz, nt=nt),
        grid=(bsz, nt + 1),
        in_specs=[pl.BlockSpec((None, tm, d), lambda b, t: (b, jnp.minimum(t, nt - 1), 0)),
                  pl.BlockSpec((None, ctx_len, d), lambda b, t: (b, 0, 0)),
                  pl.BlockSpec(mod.shape, lambda b, t: (0, 0)),
                  pl.BlockSpec(norm_g.shape, lambda b, t: (0, 0))],
        out_specs=pl.BlockSpec((None, tm, d), lambda b, t: (b, t, 0)),
        out_shape=jax.ShapeDtypeStruct((bsz, length + ctx_len, d), BF16),
        compiler_params=_params(("arbitrary", "arbitrary"), 40),
        name="norm_mod",
    )(x, ctx, mod, norm_g)


def _w_in_kernel(wt_ref, o16_ref, o32_ref, *, d):
    o = [0]
    for width in (Q_LORA, KV_LORA, QK_ROPE, 1024, 1024, 1024, 1024, 1024, 2 * d):
        o.append(o[-1] + width)

    def grab(lo, hi):
        return jnp.transpose(wt_ref[lo:hi, :]).astype(BF16)

    step = 1024
    for c in range(0, 2 * d, step):
        o16_ref[:, COL_GATES + c:COL_GATES + c + step] = grab(o[8] + c, o[8] + c + step)
    o16_ref[:, COL_HQ:COL_HQ + 1024] = grab(o[3], o[4])
    o16_ref[:, COL_HI:COL_HI + 1024] = grab(o[6], o[7])
    o16_ref[:, COL_HG:COL_HG + 1024] = grab(o[7], o[8])
    o16_ref[:, COL_CQ:COL_KR] = grab(o[0], o[2])
    kr = grab(o[2], o[2] + LANES)[:, 0:QK_ROPE]
    half = QK_ROPE // 2
    tail = jnp.zeros((kr.shape[0], N16 - COL_KR - 2 * QK_ROPE), BF16)
    o16_ref[:, COL_KR:N16] = jnp.concatenate([kr, -kr[:, half:], kr[:, :half], tail], axis=1)
    o32_ref[:, 0:1024] = grab(o[4], o[5])
    o32_ref[:, 1024:2048] = grab(o[5], o[6])


def _w_in_prep(w3d_t):
    _, n, k = w3d_t.shape
    tk = _pick(k, (256, 128))
    return pl.pallas_call(
        functools.partial(_w_in_kernel, d=k),
        grid=(k // tk,),
        in_specs=[pl.BlockSpec((None, n, tk), lambda i: (0, 0, i))],
        out_specs=[pl.BlockSpec((tk, N16), lambda i: (i, 0)),
                   pl.BlockSpec((tk, 2 * HEADS * HG_DIM), lambda i: (i, 0))],
        out_shape=[jax.ShapeDtypeStruct((k, N16), BF16), jax.ShapeDtypeStruct((k, 2 * HEADS * HG_DIM), BF16)],
        compiler_params=_params(("arbitrary",), 48),
        name="w_in_prep",
    )(w3d_t)


def _matmul_kernel(a_ref, w_ref, o_ref):
    o_ref[...] = jnp.dot(a_ref[...], w_ref[...], preferred_element_type=F32).astype(o_ref.dtype)


def _matmul(a, w, out_dtype, name):
    m, k = a.shape
    n = w.shape[1]
    tm = _pick(m, (1024, 512, 256))
    tn = _pick(n, (2048, 1024, 512, 256))
    return pl.pallas_call(
        _matmul_kernel,
        grid=(m // tm, n // tn),
        in_specs=[pl.BlockSpec((tm, k), lambda i, j: (i, 0)),
                  pl.BlockSpec((k, tn), lambda i, j: (0, j))],
        out_specs=pl.BlockSpec((tm, tn), lambda i, j: (i, j)),
        out_shape=jax.ShapeDtypeStruct((m, n), out_dtype),
        compiler_params=_params(("arbitrary", "arbitrary"), 48),
        name=name,
    )(a, w)


def _mla_prep_kernel(cq_ref, ckv_ref, kr_ref, cos_ref, sin_ref, cost_ref, sint_ref, gq_ref, gkv_ref,
                     wqmt_ref, wk_ref, wvt_ref, qt_ref, k_ref, vt_ref, *, q_scale, n_lat):
    t = pl.program_id(1)
    nt_dims = (((1,), (1,)), ((), ()))
    half = QK_ROPE // 2

    @pl.when(t < n_lat)
    def _():
        cqn = (_rms(cq_ref[...].astype(F32)) * gq_ref[...]).astype(BF16)
        main_t = lax.dot_general(wqmt_ref[...], cqn, nt_dims, preferred_element_type=F32)
        cos_t = cost_ref[...]
        sin_t = sint_ref[...]
        for h in range(HEADS):
            r0 = h * QK_PAD
            qt_ref[h, 0:QK_NOPE, :] = (main_t[r0:r0 + QK_NOPE, :] * q_scale).astype(BF16)
            rope = main_t[r0 + QK_NOPE:r0 + QK_PAD, :]
            swapped = jnp.concatenate([rope[half:2 * half], rope[0:half], rope[2 * half:]], axis=0)
            qt_ref[h, QK_NOPE:QK_PAD, :] = ((rope * cos_t + swapped * sin_t) * q_scale).astype(BF16)

    cos = cos_ref[...]
    sin = sin_ref[...]
    ckvn = (_rms(ckv_ref[...].astype(F32)) * gkv_ref[...]).astype(BF16)
    kn = jnp.dot(ckvn, wk_ref[...], preferred_element_type=F32)
    vt = lax.dot_general(wvt_ref[...], ckvn, nt_dims, preferred_element_type=F32)
    tm = vt.shape[1]
    ones_rows = (lax.broadcasted_iota(jnp.int32, (VT_ROWS - V_HEAD, tm), 0) == 0).astype(BF16)
    lane = lax.broadcasted_iota(jnp.int32, cos.shape, 1)
    a = kr_ref[...].astype(F32) * jnp.where(lane < QK_ROPE, cos, sin)
    k_rope = (a + pltpu.roll(a, QK_ROPE, axis=1)).astype(BF16)
    for h in range(HEADS):
        c0 = h * QK_PAD
        k_ref[:, c0:c0 + QK_NOPE] = kn[:, h * QK_NOPE:(h + 1) * QK_NOPE].astype(BF16)
        k_ref[:, c0 + QK_NOPE:c0 + QK_PAD] = k_rope
        vt_ref[h, 0:V_HEAD, :] = vt[h * V_HEAD:(h + 1) * V_HEAD, :].astype(BF16)
        vt_ref[h, V_HEAD:VT_ROWS, :] = ones_rows


def _mla_prep(p16, tables, gq, gkv, wqm_t, wk, wvt, ctx_len):
    bsz, lk, _ = p16.shape
    tm = KV_TILE
    assert ctx_len == tm
    nt = lk // tm
    n_lat = nt - 1
    cos_t, sin_t, cos_tt, sin_tt = tables
    whole = lambda arr: pl.BlockSpec(arr.shape, lambda b, t: (0,) * arr.ndim)
    return pl.pallas_call(
        functools.partial(_mla_prep_kernel, q_scale=float((QK_NOPE + QK_ROPE) ** -0.5 * LOG2E), n_lat=n_lat),
        grid=(bsz, nt),
        in_specs=[pl.BlockSpec((None, tm, Q_LORA), lambda b, t: (b, t, COL_CQ // Q_LORA)),
                  pl.BlockSpec((None, tm, KV_LORA), lambda b, t: (b, t, COL_CKV // KV_LORA)),
                  pl.BlockSpec((None, tm, LANES), lambda b, t: (b, t, COL_KR // LANES)),
                  pl.BlockSpec((tm, LANES), lambda b, t: (t, 0)),
                  pl.BlockSpec((tm, LANES), lambda b, t: (t, 0)),
                  pl.BlockSpec((LANES, tm), lambda b, t: (0, t)),
                  pl.BlockSpec((LANES, tm), lambda b, t: (0, t)),
                  whole(gq), whole(gkv), whole(wqm_t), whole(wk), whole(wvt)],
        out_specs=[pl.BlockSpec((None, HEADS, None, QK_PAD, tm), lambda b, t: (b, 0, jnp.minimum(t, n_lat - 1), 0, 0)),
                   pl.BlockSpec((None, tm, HEADS * QK_PAD), lambda b, t: (b, t, 0)),
                   pl.BlockSpec((None, HEADS, None, VT_ROWS, tm), lambda b, t: (b, 0, t, 0, 0))],
        out_shape=[jax.ShapeDtypeStruct((bsz, HEADS, n_lat, QK_PAD, tm), BF16),
                   jax.ShapeDtypeStruct((bsz, lk, HEADS * QK_PAD), BF16),
                   jax.ShapeDtypeStruct((bsz, HEADS, nt, VT_ROWS, tm), BF16)],
        compiler_params=_params(("arbitrary", "arbitrary"), 40),
        name="mla_prep",
    )(p16, p16, p16, cos_t, sin_t, cos_tt, sin_tt, gq, gkv, wqm_t, wk, wvt)


def _attn_kernel(qt_ref, k_ref, vt_ref, o_ref, s_ref, acc_ref, *, tq, kt, nk):
    tk = kt * KV_TILE
    sub = tq // KV_TILE
    nq = qt_ref.shape[0] // sub
    q_cols = [slice(h * KV_TILE, (h + 1) * KV_TILE) for h in range(sub)]
    acc_ref[...] = jnp.zeros(acc_ref.shape, F32)

    def scores(i, j, slot):
        r0 = pl.multiple_of(j * tk, tk)
        ks = k_ref[pl.ds(r0, tk), :]
        cmax = []
        for h, qs in enumerate(q_cols):
            s = jnp.dot(ks, qt_ref[i * sub + h], preferred_element_type=F32)
            s_ref[slot, :, qs] = s
            cmax.append(jnp.max(s, axis=0, keepdims=True))
        return jnp.concatenate(cmax, axis=1)

    def accumulate(j, slot, m_old, cmax):
        m_new = jnp.maximum(m_old, cmax)
        alpha = jnp.exp2(m_old - m_new)
        for qs in q_cols:
            pv = None
            for c in range(kt):
                p = jnp.exp2(s_ref[slot, c * KV_TILE:(c + 1) * KV_TILE, qs] - m_new[:, qs]).astype(BF16)
                d = jnp.dot(vt_ref[j * kt + c], p, preferred_element_type=F32)
                pv = d if pv is None else pv + d
            acc_ref[:, qs] = alpha[:, qs] * acc_ref[:, qs] + pv
        return m_new

    neg_inf = jnp.full((1, tq), -jnp.inf, F32)

    def tile(i, cm0):
        def triple(t, carry):
            m, cm0 = carry
            cm1 = scores(i, 3 * t + 1, 1)
            m = accumulate(3 * t, 0, m, cm0)
            cm2 = scores(i, 3 * t + 2, 2)
            m = accumulate(3 * t + 1, 1, m, cm1)
            cm0 = scores(i, 3 * t + 3, 0)
            m = accumulate(3 * t + 2, 2, m, cm2)
            return m, cm0

        m, cm0 = lax.fori_loop(0, (nk - 2) // 3, triple, (neg_inf, cm0), unroll=True)
        cm1 = scores(i, nk - 1, 1)
        m = accumulate(nk - 2, 0, m, cm0)
        cm_next = scores(jnp.minimum(i + 1, nq - 1), 0, 0)
        accumulate(nk - 1, 1, m, cm1)
        acc = acc_ref[...]
        o = acc[0:V_HEAD, :] / acc[V_HEAD:V_HEAD + 1, :]
        o_ref[pl.ds(pl.multiple_of(i * tq, tq), tq), :] = jnp.transpose(o.astype(o_ref.dtype))
        return cm_next

    lax.fori_loop(0, nq, tile, scores(0, 0, 0))


def _attention(qt, k, vt):
    bsz, heads, n_lat, _, _ = qt.shape
    length = n_lat * KV_TILE
    lk = k.shape[1]
    nc = vt.shape[2]
    tq = _pick(length, (1024,))
    kt = _pick(nc, (3, 1))
    assert (nc // kt) % 3 == 2, "the three-slot chunk pipeline peels two trailing chunks per query tile"
    return pl.pallas_call(
        functools.partial(_attn_kernel, tq=tq, kt=kt, nk=nc // kt),
        grid=(bsz, heads),
        in_specs=[pl.BlockSpec((None, None, n_lat, QK_PAD, KV_TILE), lambda b, h: (b, h, 0, 0, 0)),
                  pl.BlockSpec((None, lk, QK_PAD), lambda b, h: (b, 0, h)),
                  pl.BlockSpec((None, None, nc, VT_ROWS, KV_TILE), lambda b, h: (b, h, 0, 0, 0))],
        out_specs=pl.BlockSpec((None, length, V_HEAD), lambda b, h: (b, 0, h)),
        out_shape=jax.ShapeDtypeStruct((bsz, length, heads * V_HEAD), BF16),
        scratch_shapes=[pltpu.VMEM((3, kt * KV_TILE, tq), F32), pltpu.VMEM((VT_ROWS, tq), F32)],
        compiler_params=_params(("arbitrary", "arbitrary"), 48),
        name="attn",
    )(qt, k, vt)


def _hgrn_kernel(hq_ref, hi_ref, hg_ref, ff_ref, fb_ref, lb_ref, gn_ref, o_ref,
                 oacc_ref, st_ref, qd_ref, kd_ref, kt_ref, dec_ref, *, ctx_len, length):
    c = HG_CHUNK
    gsz = HG_GROUP * c
    n_groups = length // gsz
    half = n_groups // 2
    row = lax.broadcasted_iota(jnp.int32, (c, c), 0)
    col = lax.broadcasted_iota(jnp.int32, (c, c), 1)
    tri_c = (row >= col, row <= col)
    lbr = lb_ref[...]
    q_scale = float(HG_DIM ** -0.5)
    nt_dims = (((1,), (1,)), ((), ()))
    tn_dims = (((0,), (0,)), ((), ()))

    def lower_bound(direction):
        a0 = lbr[direction:direction + 1, :]
        a1 = lbr[2 + direction:3 + direction, :]
        mx = jnp.maximum(a0, a1)
        e0 = jnp.exp(a0 - mx)
        e1 = jnp.exp(a1 - mx)
        return e0 / (e0 + e1)

    lbs = (lower_bound(0), lower_bound(1))

    def chunk_order(direction, nchunks):
        return range(nchunks) if direction == 0 else range(nchunks - 1, -1, -1)

    def gates(direction, r0, nchunks):
        n = nchunks * c
        f_ref = ff_ref if direction == 0 else fb_ref
        lb = lbs[direction]
        f = lb + (1.0 - lb) * jax.nn.sigmoid(f_ref[pl.ds(r0, n), :])
        kk = 1.0 - f
        lf = jnp.log(f) * LOG2E
        pos = lax.broadcasted_iota(jnp.int32, (n, HG_DIM), 0) & (c - 1)
        bcum = lf
        for sh in (1, 2, 4, 8, 16, 32):
            if direction == 0:
                bcum = bcum + jnp.where(pos >= sh, pltpu.roll(bcum, sh, axis=0), 0.0)
            else:
                bcum = bcum + jnp.where(pos < c - sh, pltpu.roll(bcum, n - sh, axis=0), 0.0)
        last = c - 1 if direction == 0 else 0
        b_last = [bcum[j * c + last:j * c + last + 1, :] for j in range(nchunks)]
        b_last_rows = jnp.concatenate([jnp.broadcast_to(bl, (c, HG_DIM)) for bl in b_last], axis=0)
        k_tail = (kk * jnp.exp2(b_last_rows - bcum)).astype(BF16)
        return kk, bcum, k_tail, jnp.exp2(jnp.concatenate(b_last, axis=0))

    def state_update(st, v, k_tail, decay_row):
        u_t = lax.dot_general(v, k_tail, tn_dims, preferred_element_type=F32)
        return st * decay_row + u_t

    for direction in (0, 1):
        nchunks = ctx_len // c
        _, _, k_tail, decay = gates(direction, length, nchunks)
        st = jnp.zeros((HG_DIM, HG_DIM), F32)
        for j in chunk_order(direction, nchunks):
            sl = slice(j * c, (j + 1) * c)
            st = state_update(st, hi_ref[length + j * c:length + (j + 1) * c, :], k_tail[sl], decay[j:j + 1, :])
        st_ref[direction] = st

    def groups_of(s):
        return (s, n_groups - 1 - s)

    pc = 2
    pr = pc * c
    units = HG_GROUP // pc

    def prepare_unit(s, slot, direction, u):
        g = groups_of(s)[direction]
        r0 = pl.multiple_of(g * gsz + u * pr, pr)
        rows = slice(u * pr, (u + 1) * pr)
        kk, bcum, k_tail, decay = gates(direction, r0, pc)
        q = _silu(hq_ref[pl.ds(r0, pr), :].astype(F32)) * q_scale
        qd_ref[slot, direction, rows, :] = (q * jnp.exp2(bcum)).astype(BF16)
        kd_ref[slot, direction, rows, :] = (kk * jnp.exp2(-bcum)).astype(BF16)
        kt_ref[slot, direction, rows, :] = k_tail
        dec_ref[slot, direction, u * pc:(u + 1) * pc, :] = decay

    def consume_unit(s, slot, direction, u, st, finalize):
        g = groups_of(s)[direction]
        o0 = pl.multiple_of(g * gsz + u * pr, pr)
        outs = [None] * pc
        for jj in chunk_order(direction, pc):
            j = u * pc + jj
            sl = slice(j * c, (j + 1) * c)
            q_d = qd_ref[slot, direction, sl, :]
            v = hi_ref[pl.ds(o0 + jj * c, c), :]
            sc = lax.dot_general(q_d, kd_ref[slot, direction, sl, :], nt_dims, preferred_element_type=F32)
            sc = jnp.where(tri_c[direction], sc, 0.0).astype(BF16)
            outs[jj] = (jnp.dot(sc, v, preferred_element_type=F32)
                        + lax.dot_general(q_d, st.astype(BF16), nt_dims, preferred_element_type=F32))
            st = state_update(st, v, kt_ref[slot, direction, sl, :], dec_ref[slot, direction, j:j + 1, :])
        out = jnp.concatenate(outs, axis=0)
        if finalize:
            tot = oacc_ref[pl.ds(o0, pr), :] + out
            gate = _silu(hg_ref[pl.ds(o0, pr), :].astype(F32))
            o_ref[pl.ds(o0, pr), :] = (_rms(tot) * gn_ref[...] * gate).astype(o_ref.dtype)
        else:
            oacc_ref[pl.ds(o0, pr), :] = out
        return st

    def run(cons, prep):
        st = [st_ref[0], st_ref[1]] if cons is not None else None
        for idx in range(units):
            for direction in (0, 1):
                u = chunk_order(direction, units)[idx]
                if cons is not None:
                    st[direction] = consume_unit(cons[0], cons[1], direction, u, st[direction], cons[2])
                if prep is not None:
                    prepare_unit(prep[0], prep[1], direction, u)
        if cons is not None:
            st_ref[0] = st[0]
            st_ref[1] = st[1]

    def pair(finalize):
        def body(i, carry):
            run((2 * i, 0, finalize), (2 * i + 1, 1))
            run((2 * i + 1, 1, finalize), (2 * i + 2, 0))
            return carry
        return body

    run(None, (0, 0))
    lax.fori_loop(0, half // 2, pair(False), 0)
    lax.fori_loop(half // 2, n_groups // 2 - 1, pair(True), 0)
    run((n_groups - 2, 0, True), (n_groups - 1, 1))
    run((n_groups - 1, 1, True), None)


def _hgrn(p16, p32, lb_rows, gn, ctx_len):
    bsz, lk, _ = p16.shape
    length = lk - ctx_len
    gsz = HG_GROUP * HG_CHUNK
    assert (length // gsz) % 4 == 0 and ctx_len <= gsz
    strip = lambda col: pl.BlockSpec((None, lk, LANES), lambda b, h: (b, 0, col // LANES + h))
    whole = lambda arr: pl.BlockSpec(arr.shape, lambda b, h: (0,) * arr.ndim)
    return pl.pallas_call(
        functools.partial(_hgrn_kernel, ctx_len=ctx_len, length=length),
        grid=(bsz, HEADS),
        in_specs=[strip(COL_HQ), strip(COL_HI), strip(COL_HG), strip(0), strip(HEADS * HG_DIM),
                  pl.BlockSpec((None, 4, LANES), lambda b, h: (h, 0, 0)),
                  whole(gn)],
        out_specs=pl.BlockSpec((None, length, LANES), lambda b, h: (b, 0, h)),
        out_shape=jax.ShapeDtypeStruct((bsz, length, HEADS * HG_DIM), BF16),
        scratch_shapes=[pltpu.VMEM((length, HG_DIM), F32), pltpu.VMEM((2, HG_DIM, HG_DIM), F32),
                        pltpu.VMEM((2, 2, gsz, HG_DIM), BF16), pltpu.VMEM((2, 2, gsz, HG_DIM), BF16),
                        pltpu.VMEM((2, 2, gsz, HG_DIM), BF16), pltpu.VMEM((2, 2, HG_GROUP, HG_DIM), F32)],
        compiler_params=_params(("arbitrary", "arbitrary"), 48),
        name="hgrn",
    )(p16, p16, p16, p32, p32, lb_rows, gn)


def _merge_kernel(attn_ref, hg_ref, gl_ref, x_ref, mod_ref, ng_ref, wa_ref, wh_ref, wo_ref, x1_ref, h2_ref, *, d):
    b = pl.program_id(0)
    gt_a = mod_ref[pl.ds(b, 1), 2 * d:3 * d]
    sh_f = mod_ref[pl.ds(b, 1), 3 * d:4 * d]
    sc_f = mod_ref[pl.ds(b, 1), 4 * d:5 * d]
    gain_a = ng_ref[1:2, :] * gt_a
    gain_f = ng_ref[2:3, :] * (1.0 + sc_f)
    rows_half = attn_ref.shape[0] // 2
    for r0 in (0, rows_half):
        rs = slice(r0, r0 + rows_half)
        ya = jnp.dot(attn_ref[rs, :], wa_ref[...], preferred_element_type=F32)
        yh = jnp.dot(hg_ref[rs, :], wh_ref[...], preferred_element_type=F32)
        g0 = jax.nn.sigmoid(gl_ref[rs, 0:d].astype(F32))
        g1 = jax.nn.sigmoid(gl_ref[rs, d:2 * d].astype(F32))
        z = (g0 * ya + g1 * yh).astype(BF16)
        o = jnp.dot(z, wo_ref[...], preferred_element_type=F32)
        x1 = x_ref[rs, :] + _rms(o) * gain_a
        x1_ref[rs, :] = x1
        h2_ref[rs, :] = (_rms(x1) * gain_f + sh_f).astype(BF16)


def _merge(attn, hgn, p16, x, mod, norm_g, w_bm, w_bh, w_o):
    bsz, length, d = x.shape
    tm = _pick(length, (512, 256))
    whole = lambda arr: pl.BlockSpec(arr.shape, lambda b, i: (0,) * arr.ndim, pipeline_mode=pl.Buffered(1))
    return pl.pallas_call(
        functools.partial(_merge_kernel, d=d),
        grid=(bsz, length // tm),
        in_specs=[pl.BlockSpec((None, tm, attn.shape[2]), lambda b, i: (b, i, 0)),
                  pl.BlockSpec((None, tm, hgn.shape[2]), lambda b, i: (b, i, 0)),
                  pl.BlockSpec((None, tm, 2 * d), lambda b, i: (b, i, COL_GATES // (2 * d))),
                  pl.BlockSpec((None, tm, d), lambda b, i: (b, i, 0)),
                  whole(mod), whole(norm_g), whole(w_bm), whole(w_bh), whole(w_o)],
        out_specs=[pl.BlockSpec((None, tm, d), lambda b, i: (b, i, 0)),
                   pl.BlockSpec((None, tm, d), lambda b, i: (b, i, 0))],
        out_shape=[jax.ShapeDtypeStruct((bsz, length, d), F32),
                   jax.ShapeDtypeStruct((bsz, length, d), BF16)],
        compiler_params=_params(("arbitrary", "arbitrary"), 56),
        name="merge",
    )(attn, hgn, p16, x, mod, norm_g, w_bm, w_bh, w_o)


def _ffn_kernel(h_ref, wa_ref, wb_ref, wo_ref, x1_ref, mod_ref, ng_ref, o_ref, acc_ref, *, d, nf):
    b = pl.program_id(0)
    f = pl.program_id(2)

    @pl.when(f == 0)
    def _():
        acc_ref[...] = jnp.zeros(acc_ref.shape, F32)

    h = h_ref[...]
    a = jnp.dot(h, wa_ref[...], preferred_element_type=F32)
    g = jnp.dot(h, wb_ref[...], preferred_element_type=F32)
    act = (_silu(a) * g).astype(BF16)
    acc_ref[...] += jnp.dot(act, wo_ref[...], preferred_element_type=F32)

    @pl.when(f == nf - 1)
    def _():
        gain = ng_ref[3:4, :] * mod_ref[pl.ds(b, 1), 5 * d:6 * d]

        def slab(r, carry):
            rows = pl.ds(pl.multiple_of(r * NORM_SLAB, NORM_SLAB), NORM_SLAB)
            o_ref[rows, :] = x1_ref[rows, :] + _rms(acc_ref[rows, :]) * gain
            return carry

        lax.fori_loop(0, o_ref.shape[0] // NORM_SLAB, slab, 0, unroll=4)


def _ffn(h2, x1, mod, norm_g, w_fi, w_fo):
    bsz, length, d = x1.shape
    d_ff = w_fo.shape[0]
    tm = _pick(length, (512, 256))
    tf = w_fi.shape[2]
    nf = d_ff // tf
    whole = lambda arr: pl.BlockSpec(arr.shape, lambda b, i, f: (0,) * arr.ndim)
    return pl.pallas_call(
        functools.partial(_ffn_kernel, d=d, nf=nf),
        grid=(bsz, length // tm, nf),
        in_specs=[pl.BlockSpec((None, tm, d), lambda b, i, f: (b, i, 0)),
                  pl.BlockSpec((None, d, tf), lambda b, i, f: (f, 0, 0)),
                  pl.BlockSpec((None, d, tf), lambda b, i, f: (nf + f, 0, 0)),
                  pl.BlockSpec((tf, d), lambda b, i, f: (f, 0)),
                  pl.BlockSpec((None, tm, d), lambda b, i, f: (b, i, 0)),
                  whole(mod), whole(norm_g)],
        out_specs=pl.BlockSpec((None, tm, d), lambda b, i, f: (b, i, 0)),
        out_shape=jax.ShapeDtypeStruct((bsz, length, d), F32),
        scratch_shapes=[pltpu.VMEM((tm, d), F32)],
        compiler_params=_params(("arbitrary", "arbitrary", "arbitrary"), 48),
        name="ffn",
    )(h2, w_fi, w_fi, w_fo, x1, mod, norm_g)


def _rope_tables(length, ctx_len):
    rows = length // GRID_W
    pairs = QK_ROPE // 4
    row = np.repeat(np.arange(rows, dtype=np.float64), GRID_W)
    col = np.tile(np.arange(GRID_W, dtype=np.float64), rows)
    inv = ROPE_THETA ** (-np.arange(pairs, dtype=np.float64) / pairs)
    ang = np.concatenate([row[:, None] * inv, col[:, None] * inv], axis=-1)
    rep = LANES // (QK_ROPE // 2)
    cos = np.concatenate([np.tile(np.cos(ang), (1, rep)), np.ones((ctx_len, LANES))], axis=0).astype(np.float32)
    sin = np.concatenate([np.tile(np.sin(ang), (1, rep)), np.zeros((ctx_len, LANES))], axis=0).astype(np.float32)
    sign = np.where((np.arange(LANES) % QK_ROPE) < QK_ROPE // 2, -1.0, 1.0).astype(np.float32)
    sin_signed_t = np.ascontiguousarray(sin.T * sign[:, None])
    return tuple(jnp.asarray(t) for t in (cos, sin, np.ascontiguousarray(cos.T), sin_signed_t))


def kernel(x, c, ctx, c_ctx, w_mod, b_mod, norm_g, w_in, mla_q_norm, mla_kv_norm, w_uq, w_ukv,
           hgrn_lb, hgrn_o_norm, w_br_mla, w_br_hgrn, w_out, w_ffn_in, w_ffn_out):
    bsz, length, d = x.shape
    ctx_len = ctx.shape[1]
    depth = w_mod.shape[0]
    assert depth == 1 and hgrn_lb.shape[0] == 2, "kernel is written for the depth-1 block"
    assert bsz < 8 and length % (HG_GROUP * HG_CHUNK) == 0 and ctx_len % HG_CHUNK == 0
    lk = length + ctx_len

    w16, w32 = _w_in_prep(jnp.swapaxes(w_in, 1, 2))

    wq = w_uq[0].reshape(Q_LORA, HEADS, QK_NOPE + QK_ROPE)
    wq_nope, wq_rope = wq[..., :QK_NOPE], wq[..., QK_NOPE:]
    zpad = jnp.zeros((Q_LORA, HEADS, QK_PAD - QK_NOPE - QK_ROPE), F32)
    wqm_t = jnp.concatenate([wq_nope, wq_rope, zpad], axis=-1).reshape(Q_LORA, HEADS * QK_PAD).T.astype(BF16)
    wkv = w_ukv[0].reshape(KV_LORA, HEADS, QK_NOPE + V_HEAD)
    wk = wkv[..., :QK_NOPE].reshape(KV_LORA, HEADS * QK_NOPE).astype(BF16)
    wvt = wkv[..., QK_NOPE:].reshape(KV_LORA, HEADS * V_HEAD).T.astype(BF16)
    w_bm = w_br_mla[0].astype(BF16)
    w_bh = w_br_hgrn[0].astype(BF16)
    w_o = w_out[0].astype(BF16)
    tf = _pick(w_ffn_out.shape[1], (FFN_TILE,))
    w_fi = w_ffn_in[0].reshape(d, -1, tf).transpose(1, 0, 2).astype(BF16)
    w_fo = w_ffn_out[0].astype(BF16)
    lb_rows = hgrn_lb.transpose(2, 0, 1, 3).reshape(HEADS, 4, HG_DIM)
    tables = _rope_tables(length, ctx_len)

    cc = jnp.concatenate([c, c_ctx[None, :], jnp.zeros((8 - bsz - 1, d), F32)], axis=0)
    mod = _mod(cc, w_mod[0], b_mod[0][None, :])
    ng = norm_g[0]

    hcat = _norm_mod(x, ctx, mod, ng)
    h2d = hcat.reshape(bsz * lk, d)
    p16 = _matmul(h2d, w16, BF16, "in_proj16").reshape(bsz, lk, N16)
    p32 = _matmul(h2d, w32, F32, "in_proj32").reshape(bsz, lk, 2 * HEADS * HG_DIM)
    qt, k, vt = _mla_prep(p16, tables, mla_q_norm[0][None, :], mla_kv_norm[0][None, :],
                          wqm_t, wk, wvt, ctx_len)
    attn = _attention(qt, k, vt)
    hgn = _hgrn(p16, p32, lb_rows, hgrn_o_norm[0][None, :], ctx_len)
    x1, h2 = _merge(attn, hgn, p16, x, mod, ng, w_bm, w_bh, w_o)
    return _ffn(h2, x1, mod, ng, w_fi, w_fo)
```

```python
import functools

import jax
import jax.numpy as jnp
import numpy as np
from jax import lax
from jax.experimental import pallas as pl
from jax.experimental.pallas import tpu as pltpu

F32 = jnp.float32
BF16 = jnp.bfloat16

RMS_EPS = 1e-6
N_MOD = 6
GRID_W = 64
ROPE_THETA = 10000.0
HEADS = 8
Q_LORA = 512
KV_LORA = 256
QK_NOPE = 128
QK_ROPE = 64
V_HEAD = 128
QK_PAD = 256
HG_DIM = 128
HG_CHUNK = 64
HG_GROUP = 8
LANES = 128
NORM_SLAB = 16
KV_TILE = 256
VT_ROWS = 144
LOG2E = 1.4426950408889634
MIB = 1024 * 1024

COL_GATES = 0
COL_HQ = 4096
COL_HI = 5120
COL_HG = 6144
COL_CQ = 7168
COL_CKV = 7680
COL_KR = 7936
N16 = 8192


def _pick(n, candidates):
    for c in candidates:
        if n % c == 0:
            return c
    raise ValueError(f"no tile for {n} in {candidates}")


def _params(sem, vmem_mib):
    return pltpu.CompilerParams(dimension_semantics=sem, vmem_limit_bytes=vmem_mib * MIB)


def _rms(xf):
    return xf * lax.rsqrt(jnp.mean(xf * xf, axis=-1, keepdims=True) + RMS_EPS)


def _silu(x):
    return x * jax.nn.sigmoid(x)


def _mod_kernel(c_ref, w_ref, b_ref, o_ref):
    s = _silu(c_ref[...]).astype(BF16)
    o_ref[...] = jnp.dot(s, w_ref[...].astype(BF16), preferred_element_type=F32) + b_ref[...]


def _mod(cc, w_mod, b_mod):
    d, n = w_mod.shape
    tn = _pick(n, (1024, 512, 256, 128))
    return pl.pallas_call(
        _mod_kernel,
        grid=(n // tn,),
        in_specs=[pl.BlockSpec((8, d), lambda j: (0, 0)),
                  pl.BlockSpec((d, tn), lambda j: (0, j)),
                  pl.BlockSpec((1, tn), lambda j: (0, j))],
        out_specs=pl.BlockSpec((8, tn), lambda j: (0, j)),
        out_shape=jax.ShapeDtypeStruct((8, n), F32),
        compiler_params=_params(("arbitrary",), 40),
        name="mod",
    )(cc, w_mod, b_mod)


def _norm_mod_kernel(x_ref, ctx_ref, mod_ref, g_ref, o_ref, *, d, ctx_row, nt):
    b = pl.program_id(0)
    t = pl.program_id(1)
    g = g_ref[0:1, :]

    def emit(src_ref, row):
        sh = mod_ref[pl.ds(row, 1), 0:d]
        sc1 = 1.0 + mod_ref[pl.ds(row, 1), d:2 * d]

        def slab(r, carry):
            rows = pl.ds(pl.multiple_of(r * NORM_SLAB, NORM_SLAB), NORM_SLAB)
            o_ref[rows, :] = (_rms(src_ref[rows, :]) * g * sc1 + sh).astype(BF16)
            return carry

        lax.fori_loop(0, src_ref.shape[0] // NORM_SLAB, slab, 0, unroll=4)

    @pl.when(t < nt)
    def _():
        emit(x_ref, b)

    @pl.when(t == nt)
    def _():
        emit(ctx_ref, ctx_row)


def _norm_mod(x, ctx, mod, norm_g):
    bsz, length, d = x.shape
    ctx_len = ctx.shape[1]
    tm = _pick(length, (1024, 512, 256))
    nt = length // tm
    assert ctx_len <= tm
    return pl.pallas_call(
        functools.partial(_norm_mod_kernel, d=d, ctx_row=bsz, nt=nt),
        grid=(bsz, nt + 1),
        in_specs=[pl.BlockSpec((None, tm, d), lambda b, t: (b, jnp.minimum(t, nt - 1), 0)),
                  pl.BlockSpec((None, ctx_len, d), lambda b, t: (b, 0, 0)),
                  pl.BlockSpec(mod.shape, lambda b, t: (0, 0)),
                  pl.BlockSpec(norm_g.shape, lambda b, t: (0, 0))],
        out_specs=pl.BlockSpec((None, tm, d), lambda b, t: (b, t, 0)),
        out_shape=jax.ShapeDtypeStruct((bsz, length + ctx_len, d), BF16),
        compiler_params=_params(("arbitrary", "arbitrary"), 40),
        name="norm_mod",
    )(x, ctx, mod, norm_g)


def _w_in_kernel(wt_ref, o16_ref, o32_ref, *, d):
    o = [0]
    for width in (Q_LORA, KV_LORA, QK_ROPE, 1024, 1024, 1024, 1024, 1024, 2 * d):
        o.append(o[-1] + width)

    def grab(lo, hi):
        return jnp.transpose(wt_ref[lo:hi, :]).astype(BF16)

    step = 1024
    for c in range(0, 2 * d, step):
        o16_ref[:, COL_GATES + c:COL_GATES + c + step] = grab(o[8] + c, o[8] + c + step)
    o16_ref[:, COL_HQ:COL_HQ + 1024] = grab(o[3], o[4])
    o16_ref[:, COL_HI:COL_HI + 1024] = grab(o[6], o[7])
    o16_ref[:, COL_HG:COL_HG + 1024] = grab(o[7], o[8])
    o16_ref[:, COL_CQ:COL_KR] = grab(o[0], o[2])
    kr = grab(o[2], o[2] + LANES)[:, 0:QK_ROPE]
    half = QK_ROPE // 2
    tail = jnp.zeros((kr.shape[0], N16 - COL_KR - 2 * QK_ROPE), BF16)
    o16_ref[:, COL_KR:N16] = jnp.concatenate([kr, -kr[:, half:], kr[:, :half], tail], axis=1)
    o32_ref[:, 0:1024] = grab(o[4], o[5])
    o32_ref[:, 1024:2048] = grab(o[5], o[6])


def _w_in_prep(w3d_t):
    _, n, k = w3d_t.shape
    tk = _pick(k, (256, 128))
    return pl.pallas_call(
        functools.partial(_w_in_kernel, d=k),
        grid=(k // tk,),
        in_specs=[pl.BlockSpec((None, n, tk), lambda i: (0, 0, i))],
        out_specs=[pl.BlockSpec((tk, N16), lambda i: (i, 0)),
                   pl.BlockSpec((tk, 2 * HEADS * HG_DIM), lambda i: (i, 0))],
        out_shape=[jax.ShapeDtypeStruct((k, N16), BF16), jax.ShapeDtypeStruct((k, 2 * HEADS * HG_DIM), BF16)],
        compiler_params=_params(("arbitrary",), 48),
        name="w_in_prep",
    )(w3d_t)


def _matmul_kernel(a_ref, w_ref, o_ref):
    o_ref[...] = jnp.dot(a_ref[...], w_ref[...], preferred_element_type=F32).astype(o_ref.dtype)


def _matmul(a, w, out_dtype, name):
    m, k = a.shape
    n = w.shape[1]
    tm = _pick(m, (1024, 512, 256))
    tn = _pick(n, (2048, 1024, 512, 256))
    return pl.pallas_call(
        _matmul_kernel,
        grid=(m // tm, n // tn),
        in_specs=[pl.BlockSpec((tm, k), lambda i, j: (i, 0)),
                  pl.BlockSpec((k, tn), lambda i, j: (0, j))],
        out_specs=pl.BlockSpec((tm, tn), lambda i, j: (i, j)),
        out_shape=jax.ShapeDtypeStruct((m, n), out_dtype),
        compiler_params=_params(("arbitrary", "arbitrary"), 48),
        name=name,
    )(a, w)


def _matmul_strips_kernel(a_ref, w_ref, o_ref):
    res = jnp.dot(a_ref[...], w_ref[...], preferred_element_type=F32)
    for s in range(o_ref.shape[0]):
        o_ref[s] = res[:, s * LANES:(s + 1) * LANES]


def _matmul_strips(a, w, name):
    m, k = a.shape
    n = w.shape[1]
    tm = _pick(m, (1024, 512, 256))
    return pl.pallas_call(
        _matmul_strips_kernel,
        grid=(m // tm,),
        in_specs=[pl.BlockSpec((tm, k), lambda i: (i, 0)),
                  pl.BlockSpec((k, n), lambda i: (0, 0))],
        out_specs=pl.BlockSpec((n // LANES, tm, LANES), lambda i: (0, i, 0)),
        out_shape=jax.ShapeDtypeStruct((n // LANES, m, LANES), F32),
        compiler_params=_params(("arbitrary",), 48),
        name=name,
    )(a, w)


def _mla_prep_kernel(cq_ref, ckv_ref, kr_ref, cos_ref, sin_ref, cost_ref, sint_ref, gq_ref, gkv_ref,
                     wqmt_ref, wk_ref, wvt_ref, qt_ref, k_ref, vt_ref, *, q_scale, n_lat):
    t = pl.program_id(1)
    nt_dims = (((1,), (1,)), ((), ()))
    half = QK_ROPE // 2

    @pl.when(t < n_lat)
    def _():
        cqn = (_rms(cq_ref[...].astype(F32)) * gq_ref[...]).astype(BF16)
        main_t = lax.dot_general(wqmt_ref[...], cqn, nt_dims, preferred_element_type=F32)
        cos_t = cost_ref[...]
        sin_t = sint_ref[...]
        for h in range(HEADS):
            r0 = h * QK_PAD
            qt_ref[h, 0:QK_NOPE, :] = (main_t[r0:r0 + QK_NOPE, :] * q_scale).astype(BF16)
            rope = main_t[r0 + QK_NOPE:r0 + QK_PAD, :]
            swapped = jnp.concatenate([rope[half:2 * half], rope[0:half], rope[2 * half:]], axis=0)
            qt_ref[h, QK_NOPE:QK_PAD, :] = ((rope * cos_t + swapped * sin_t) * q_scale).astype(BF16)

    cos = cos_ref[...]
    sin = sin_ref[...]
    ckvn = (_rms(ckv_ref[...].astype(F32)) * gkv_ref[...]).astype(BF16)
    kn = jnp.dot(ckvn, wk_ref[...], preferred_element_type=F32)
    vt = lax.dot_general(wvt_ref[...], ckvn, nt_dims, preferred_element_type=F32)
    tm = vt.shape[1]
    ones_rows = (lax.broadcasted_iota(jnp.int32, (VT_ROWS - V_HEAD, tm), 0) == 0).astype(BF16)
    lane = lax.broadcasted_iota(jnp.int32, cos.shape, 1)
    a = kr_ref[...].astype(F32) * jnp.where(lane < QK_ROPE, cos, sin)
    k_rope = (a + pltpu.roll(a, QK_ROPE, axis=1)).astype(BF16)
    for h in range(HEADS):
        c0 = h * QK_PAD
        k_ref[:, c0:c0 + QK_NOPE] = kn[:, h * QK_NOPE:(h + 1) * QK_NOPE].astype(BF16)
        k_ref[:, c0 + QK_NOPE:c0 + QK_PAD] = k_rope
        vt_ref[h, 0:V_HEAD, :] = vt[h * V_HEAD:(h + 1) * V_HEAD, :].astype(BF16)
        vt_ref[h, V_HEAD:VT_ROWS, :] = ones_rows


def _mla_prep(p16, tables, gq, gkv, wqm_t, wk, wvt, ctx_len):
    bsz, lk, _ = p16.shape
    tm = KV_TILE
    assert ctx_len == tm
    nt = lk // tm
    n_lat = nt - 1
    cos_t, sin_t, cos_tt, sin_tt = tables
    whole = lambda arr: pl.BlockSpec(arr.shape, lambda b, t: (0,) * arr.ndim)
    return pl.pallas_call(
        functools.partial(_mla_prep_kernel, q_scale=float((QK_NOPE + QK_ROPE) ** -0.5 * LOG2E), n_lat=n_lat),
        grid=(bsz, nt),
        in_specs=[pl.BlockSpec((None, tm, Q_LORA), lambda b, t: (b, t, COL_CQ // Q_LORA)),
                  pl.BlockSpec((None, tm, KV_LORA), lambda b, t: (b, t, COL_CKV // KV_LORA)),
                  pl.BlockSpec((None, tm, LANES), lambda b, t: (b, t, COL_KR // LANES)),
                  pl.BlockSpec((tm, LANES), lambda b, t: (t, 0)),
                  pl.BlockSpec((tm, LANES), lambda b, t: (t, 0)),
                  pl.BlockSpec((LANES, tm), lambda b, t: (0, t)),
                  pl.BlockSpec((LANES, tm), lambda b, t: (0, t)),
                  whole(gq), whole(gkv), whole(wqm_t), whole(wk), whole(wvt)],
        out_specs=[pl.BlockSpec((None, HEADS, None, QK_PAD, tm), lambda b, t: (b, 0, jnp.minimum(t, n_lat - 1), 0, 0)),
                   pl.BlockSpec((None, tm, HEADS * QK_PAD), lambda b, t: (b, t, 0)),
                   pl.BlockSpec((None, HEADS, None, VT_ROWS, tm), lambda b, t: (b, 0, t, 0, 0))],
        out_shape=[jax.ShapeDtypeStruct((bsz, HEADS, n_lat, QK_PAD, tm), BF16),
                   jax.ShapeDtypeStruct((bsz, lk, HEADS * QK_PAD), BF16),
                   jax.ShapeDtypeStruct((bsz, HEADS, nt, VT_ROWS, tm), BF16)],
        compiler_params=_params(("arbitrary", "arbitrary"), 40),
        name="mla_prep",
    )(p16, p16, p16, cos_t, sin_t, cos_tt, sin_tt, gq, gkv, wqm_t, wk, wvt)


def _attn_kernel(qt_ref, k_ref, vt_ref, o_ref, s_ref, acc_ref, *, tq, kt, nk):
    tk = kt * KV_TILE
    sub = tq // KV_TILE
    nq = qt_ref.shape[0] // sub
    q_cols = [slice(h * KV_TILE, (h + 1) * KV_TILE) for h in range(sub)]
    acc_ref[...] = jnp.zeros(acc_ref.shape, F32)

    def scores(i, j, slot):
        r0 = pl.multiple_of(j * tk, tk)
        ks = k_ref[pl.ds(r0, tk), :]
        cmax = []
        for h, qs in enumerate(q_cols):
            s = jnp.dot(ks, qt_ref[i * sub + h], preferred_element_type=F32)
            s_ref[slot, :, qs] = s
            cmax.append(jnp.max(s, axis=0, keepdims=True))
        return jnp.concatenate(cmax, axis=1)

    def accumulate(j, slot, m_old, cmax):
        m_new = jnp.maximum(m_old, cmax)
        alpha = jnp.exp2(m_old - m_new)
        for qs in q_cols:
            pv = None
            for c in range(kt):
                p = jnp.exp2(s_ref[slot, c * KV_TILE:(c + 1) * KV_TILE, qs] - m_new[:, qs]).astype(BF16)
                d = jnp.dot(vt_ref[j * kt + c], p, preferred_element_type=F32)
                pv = d if pv is None else pv + d
            acc_ref[:, qs] = alpha[:, qs] * acc_ref[:, qs] + pv
        return m_new

    neg_inf = jnp.full((1, tq), -jnp.inf, F32)

    def tile(i, cm0):
        def triple(t, carry):
            m, cm0 = carry
            cm1 = scores(i, 3 * t + 1, 1)
            m = accumulate(3 * t, 0, m, cm0)
            cm2 = scores(i, 3 * t + 2, 2)
            m = accumulate(3 * t + 1, 1, m, cm1)
            cm0 = scores(i, 3 * t + 3, 0)
            m = accumulate(3 * t + 2, 2, m, cm2)
            return m, cm0

        m, cm0 = lax.fori_loop(0, (nk - 2) // 3, triple, (neg_inf, cm0), unroll=True)
        cm1 = scores(i, nk - 1, 1)
        m = accumulate(nk - 2, 0, m, cm0)
        cm_next = scores(jnp.minimum(i + 1, nq - 1), 0, 0)
        accumulate(nk - 1, 1, m, cm1)
        acc = acc_ref[...]
        o = acc[0:V_HEAD, :] / acc[V_HEAD:V_HEAD + 1, :]
        o_ref[pl.ds(pl.multiple_of(i * tq, tq), tq), :] = jnp.transpose(o.astype(o_ref.dtype))
        return cm_next

    lax.fori_loop(0, nq, tile, scores(0, 0, 0))


def _attention(qt, k, vt):
    bsz, heads, n_lat, _, _ = qt.shape
    length = n_lat * KV_TILE
    lk = k.shape[1]
    nc = vt.shape[2]
    tq = _pick(length, (1024,))
    kt = _pick(nc, (3, 1))
    assert (nc // kt) % 3 == 2, "the three-slot chunk pipeline peels two trailing chunks per query tile"
    return pl.pallas_call(
        functools.partial(_attn_kernel, tq=tq, kt=kt, nk=nc // kt),
        grid=(bsz, heads),
        in_specs=[pl.BlockSpec((None, None, n_lat, QK_PAD, KV_TILE), lambda b, h: (b, h, 0, 0, 0)),
                  pl.BlockSpec((None, lk, QK_PAD), lambda b, h: (b, 0, h)),
                  pl.BlockSpec((None, None, nc, VT_ROWS, KV_TILE), lambda b, h: (b, h, 0, 0, 0))],
        out_specs=pl.BlockSpec((None, length, V_HEAD), lambda b, h: (b, 0, h)),
        out_shape=jax.ShapeDtypeStruct((bsz, length, heads * V_HEAD), BF16),
        scratch_shapes=[pltpu.VMEM((3, kt * KV_TILE, tq), F32), pltpu.VMEM((VT_ROWS, tq), F32)],
        compiler_params=_params(("arbitrary", "arbitrary"), 48),
        name="attn",
    )(qt, k, vt)


def _hgrn_kernel(hq_ref, hi_ref, hg_ref, ff_ref, fb_ref, lb_ref, gn_ref, o_ref,
                 oacc_ref, st_ref, qd_ref, kd_ref, kt_ref, dec_ref, *, ctx_len, length):
    c = HG_CHUNK
    gsz = HG_GROUP * c
    n_groups = length // gsz
    half = n_groups // 2
    row = lax.broadcasted_iota(jnp.int32, (c, c), 0)
    col = lax.broadcasted_iota(jnp.int32, (c, c), 1)
    tri_c = (row >= col, row <= col)
    lbr = lb_ref[...]
    q_scale = float(HG_DIM ** -0.5)
    nt_dims = (((1,), (1,)), ((), ()))
    tn_dims = (((0,), (0,)), ((), ()))

    def lower_bound(direction):
        a0 = lbr[direction:direction + 1, :]
        a1 = lbr[2 + direction:3 + direction, :]
        mx = jnp.maximum(a0, a1)
        e0 = jnp.exp(a0 - mx)
        e1 = jnp.exp(a1 - mx)
        return e0 / (e0 + e1)

    lbs = (lower_bound(0), lower_bound(1))

    def chunk_order(direction, nchunks):
        return range(nchunks) if direction == 0 else range(nchunks - 1, -1, -1)

    def gates(direction, r0, nchunks):
        n = nchunks * c
        f_ref = ff_ref if direction == 0 else fb_ref
        lb = lbs[direction]
        f = lb + (1.0 - lb) * jax.nn.sigmoid(f_ref[pl.ds(r0, n), :])
        kk = 1.0 - f
        lf = jnp.log(f) * LOG2E
        pos = lax.broadcasted_iota(jnp.int32, (n, HG_DIM), 0) & (c - 1)
        bcum = lf
        for sh in (1, 2, 4, 8, 16, 32):
            if direction == 0:
                bcum = bcum + jnp.where(pos >= sh, pltpu.roll(bcum, sh, axis=0), 0.0)
            else:
                bcum = bcum + jnp.where(pos < c - sh, pltpu.roll(bcum, n - sh, axis=0), 0.0)
        last = c - 1 if direction == 0 else 0
        b_last = [bcum[j * c + last:j * c + last + 1, :] for j in range(nchunks)]
        b_last_rows = jnp.concatenate([jnp.broadcast_to(bl, (c, HG_DIM)) for bl in b_last], axis=0)
        k_tail = (kk * jnp.exp2(b_last_rows - bcum)).astype(BF16)
        return kk, bcum, k_tail, jnp.exp2(jnp.concatenate(b_last, axis=0))

    def state_update(st, v, k_tail, decay_row):
        u_t = lax.dot_general(v, k_tail, tn_dims, preferred_element_type=F32)
        return st * decay_row + u_t

    for direction in (0, 1):
        nchunks = ctx_len // c
        _, _, k_tail, decay = gates(direction, length, nchunks)
        st = jnp.zeros((HG_DIM, HG_DIM), F32)
        for j in chunk_order(direction, nchunks):
            sl = slice(j * c, (j + 1) * c)
            st = state_update(st, hi_ref[length + j * c:length + (j + 1) * c, :], k_tail[sl], decay[j:j + 1, :])
        st_ref[direction] = st

    def groups_of(s):
        return (s, n_groups - 1 - s)

    pc = 2
    pr = pc * c
    units = HG_GROUP // pc

    def prepare_unit(s, slot, direction, u):
        g = groups_of(s)[direction]
        r0 = pl.multiple_of(g * gsz + u * pr, pr)
        rows = slice(u * pr, (u + 1) * pr)
        kk, bcum, k_tail, decay = gates(direction, r0, pc)
        q = _silu(hq_ref[pl.ds(r0, pr), :].astype(F32)) * q_scale
        qd_ref[slot, direction, rows, :] = (q * jnp.exp2(bcum)).astype(BF16)
        kd_ref[slot, direction, rows, :] = (kk * jnp.exp2(-bcum)).astype(BF16)
        kt_ref[slot, direction, rows, :] = k_tail
        dec_ref[slot, direction, u * pc:(u + 1) * pc, :] = decay

    def consume_unit(s, slot, direction, u, st, finalize):
        g = groups_of(s)[direction]
        o0 = pl.multiple_of(g * gsz + u * pr, pr)
        outs = [None] * pc
        for jj in chunk_order(direction, pc):
            j = u * pc + jj
            sl = slice(j * c, (j + 1) * c)
            q_d = qd_ref[slot, direction, sl, :]
            v = hi_ref[pl.ds(o0 + jj * c, c), :]
            sc = lax.dot_general(q_d, kd_ref[slot, direction, sl, :], nt_dims, preferred_element_type=F32)
            sc = jnp.where(tri_c[direction], sc, 0.0).astype(BF16)
            outs[jj] = (jnp.dot(sc, v, preferred_element_type=F32)
                        + lax.dot_general(q_d, st.astype(BF16), nt_dims, preferred_element_type=F32))
            st = state_update(st, v, kt_ref[slot, direction, sl, :], dec_ref[slot, direction, j:j + 1, :])
        out = jnp.concatenate(outs, axis=0)
        if finalize:
            tot = oacc_ref[pl.ds(o0, pr), :] + out
            gate = _silu(hg_ref[pl.ds(o0, pr), :].astype(F32))
            o_ref[pl.ds(o0, pr), :] = (_rms(tot) * gn_ref[...] * gate).astype(o_ref.dtype)
        else:
            oacc_ref[pl.ds(o0, pr), :] = out
        return st

    def run(cons, prep):
        st = [st_ref[0], st_ref[1]] if cons is not None else None
        for idx in range(units):
            for direction in (0, 1):
                u = chunk_order(direction, units)[idx]
                if cons is not None:
                    st[direction] = consume_unit(cons[0], cons[1], direction, u, st[direction], cons[2])
                if prep is not None:
                    prepare_unit(prep[0], prep[1], direction, u)
        if cons is not None:
            st_ref[0] = st[0]
            st_ref[1] = st[1]

    def pair(finalize):
        def body(i, carry):
            run((2 * i, 0, finalize), (2 * i + 1, 1))
            run((2 * i + 1, 1, finalize), (2 * i + 2, 0))
            return carry
        return body

    run(None, (0, 0))
    lax.fori_loop(0, half // 2, pair(False), 0)
    lax.fori_loop(half // 2, n_groups // 2 - 1, pair(True), 0)
    run((n_groups - 2, 0, True), (n_groups - 1, 1))
    run((n_groups - 1, 1, True), None)


def _hgrn(p16, p32, lb_rows, gn, ctx_len):
    bsz, lk, _ = p16.shape
    length = lk - ctx_len
    gsz = HG_GROUP * HG_CHUNK
    assert (length // gsz) % 4 == 0 and ctx_len <= gsz
    strip = lambda col: pl.BlockSpec((None, lk, LANES), lambda b, h: (b, 0, col // LANES + h))
    whole = lambda arr: pl.BlockSpec(arr.shape, lambda b, h: (0,) * arr.ndim)
    return pl.pallas_call(
        functools.partial(_hgrn_kernel, ctx_len=ctx_len, length=length),
        grid=(bsz, HEADS),
        in_specs=[strip(COL_HQ), strip(COL_HI), strip(COL_HG),
                  pl.BlockSpec((None, lk, LANES), lambda b, h: (h, b, 0)),
                  pl.BlockSpec((None, lk, LANES), lambda b, h: (HEADS + h, b, 0)),
                  pl.BlockSpec((None, 4, LANES), lambda b, h: (h, 0, 0)),
                  whole(gn)],
        out_specs=pl.BlockSpec((None, length, LANES), lambda b, h: (b, 0, h)),
        out_shape=jax.ShapeDtypeStruct((bsz, length, HEADS * HG_DIM), BF16),
        scratch_shapes=[pltpu.VMEM((length, HG_DIM), F32), pltpu.VMEM((2, HG_DIM, HG_DIM), F32),
                        pltpu.VMEM((2, 2, gsz, HG_DIM), BF16), pltpu.VMEM((2, 2, gsz, HG_DIM), BF16),
                        pltpu.VMEM((2, 2, gsz, HG_DIM), BF16), pltpu.VMEM((2, 2, HG_GROUP, HG_DIM), F32)],
        compiler_params=_params(("arbitrary", "arbitrary"), 48),
        name="hgrn",
    )(p16, p16, p16, p32, p32, lb_rows, gn)


def _merge_kernel(attn_ref, hg_ref, gl_ref, x_ref, mod_ref, ng_ref, wa_ref, wh_ref, wo_ref, x1_ref, h2_ref, *, d):
    b = pl.program_id(0)
    gt_a = mod_ref[pl.ds(b, 1), 2 * d:3 * d]
    sh_f = mod_ref[pl.ds(b, 1), 3 * d:4 * d]
    sc_f = mod_ref[pl.ds(b, 1), 4 * d:5 * d]
    gain_a = ng_ref[1:2, :] * gt_a
    gain_f = ng_ref[2:3, :] * (1.0 + sc_f)
    rows_half = attn_ref.shape[0] // 2
    for r0 in (0, rows_half):
        rs = slice(r0, r0 + rows_half)
        ya = jnp.dot(attn_ref[rs, :], wa_ref[...], preferred_element_type=F32)
        yh = jnp.dot(hg_ref[rs, :], wh_ref[...], preferred_element_type=F32)
        g0 = jax.nn.sigmoid(gl_ref[rs, 0:d].astype(F32))
        g1 = jax.nn.sigmoid(gl_ref[rs, d:2 * d].astype(F32))
        z = (g0 * ya + g1 * yh).astype(BF16)
        o = jnp.dot(z, wo_ref[...], preferred_element_type=F32)
        x1 = x_ref[rs, :] + _rms(o) * gain_a
        x1_ref[rs, :] = x1
        h2_ref[rs, :] = (_rms(x1) * gain_f + sh_f).astype(BF16)


def _merge(attn, hgn, p16, x, mod, norm_g, w_bm, w_bh, w_o):
    bsz, length, d = x.shape
    tm = _pick(length, (512, 256))
    whole = lambda arr: pl.BlockSpec(arr.shape, lambda b, i: (0,) * arr.ndim, pipeline_mode=pl.Buffered(1))
    return pl.pallas_call(
        functools.partial(_merge_kernel, d=d),
        grid=(bsz, length // tm),
        in_specs=[pl.BlockSpec((None, tm, attn.shape[2]), lambda b, i: (b, i, 0)),
                  pl.BlockSpec((None, tm, hgn.shape[2]), lambda b, i: (b, i, 0)),
                  pl.BlockSpec((None, tm, 2 * d), lambda b, i: (b, i, COL_GATES // (2 * d))),
                  pl.BlockSpec((None, tm, d), lambda b, i: (b, i, 0)),
                  whole(mod), whole(norm_g), whole(w_bm), whole(w_bh), whole(w_o)],
        out_specs=[pl.BlockSpec((None, tm, d), lambda b, i: (b, i, 0)),
                   pl.BlockSpec((None, tm, d), lambda b, i: (b, i, 0))],
        out_shape=[jax.ShapeDtypeStruct((bsz, length, d), F32),
                   jax.ShapeDtypeStruct((bsz, length, d), BF16)],
        compiler_params=_params(("arbitrary", "arbitrary"), 56),
        name="merge",
    )(attn, hgn, p16, x, mod, norm_g, w_bm, w_bh, w_o)


def _ffn_kernel(h_ref, wa_ref, wb_ref, wo_ref, x1_ref, mod_ref, ng_ref, o_ref, acc_ref, *, d, nf):
    b = pl.program_id(0)
    f = pl.program_id(2)

    @pl.when(f == 0)
    def _():
        acc_ref[...] = jnp.zeros(acc_ref.shape, F32)

    h = h_ref[...]
    a = jnp.dot(h, wa_ref[...], preferred_element_type=F32)
    g = jnp.dot(h, wb_ref[...], preferred_element_type=F32)
    act = (_silu(a) * g).astype(BF16)
    acc_ref[...] += jnp.dot(act, wo_ref[...], preferred_element_type=F32)

    @pl.when(f == nf - 1)
    def _():
        gain = ng_ref[3:4, :] * mod_ref[pl.ds(b, 1), 5 * d:6 * d]

        def slab(r, carry):
            rows = pl.ds(pl.multiple_of(r * NORM_SLAB, NORM_SLAB), NORM_SLAB)
            o_ref[rows, :] = x1_ref[rows, :] + _rms(acc_ref[rows, :]) * gain
            return carry

        lax.fori_loop(0, o_ref.shape[0] // NORM_SLAB, slab, 0, unroll=4)


def _ffn(h2, x1, mod, norm_g, w_fi, w_fo):
    bsz, length, d = x1.shape
    d_ff = w_fo.shape[0]
    tm = _pick(length, (512, 256))
    tf = _pick(d_ff, (512, 256, 128))
    nf = d_ff // tf
    whole = lambda arr: pl.BlockSpec(arr.shape, lambda b, i, f: (0,) * arr.ndim)
    return pl.pallas_call(
        functools.partial(_ffn_kernel, d=d, nf=nf),
        grid=(bsz, length // tm, nf),
        in_specs=[pl.BlockSpec((None, tm, d), lambda b, i, f: (b, i, 0)),
                  pl.BlockSpec((d, tf), lambda b, i, f: (0, f)),
                  pl.BlockSpec((d, tf), lambda b, i, f: (0, nf + f)),
                  pl.BlockSpec((tf, d), lambda b, i, f: (f, 0)),
                  pl.BlockSpec((None, tm, d), lambda b, i, f: (b, i, 0)),
                  whole(mod), whole(norm_g)],
        out_specs=pl.BlockSpec((None, tm, d), lambda b, i, f: (b, i, 0)),
        out_shape=jax.ShapeDtypeStruct((bsz, length, d), F32),
        scratch_shapes=[pltpu.VMEM((tm, d), F32)],
        compiler_params=_params(("arbitrary", "arbitrary", "arbitrary"), 48),
        name="ffn",
    )(h2, w_fi, w_fi, w_fo, x1, mod, norm_g)


def _rope_tables(length, ctx_len):
    rows = length // GRID_W
    pairs = QK_ROPE // 4
    row = np.repeat(np.arange(rows, dtype=np.float64), GRID_W)
    col = np.tile(np.arange(GRID_W, dtype=np.float64), rows)
    inv = ROPE_THETA ** (-np.arange(pairs, dtype=np.float64) / pairs)
    ang = np.concatenate([row[:, None] * inv, col[:, None] * inv], axis=-1)
    rep = LANES // (QK_ROPE // 2)
    cos = np.concatenate([np.tile(np.cos(ang), (1, rep)), np.ones((ctx_len, LANES))], axis=0).astype(np.float32)
    sin = np.concatenate([np.tile(np.sin(ang), (1, rep)), np.zeros((ctx_len, LANES))], axis=0).astype(np.float32)
    sign = np.where((np.arange(LANES) % QK_ROPE) < QK_ROPE // 2, -1.0, 1.0).astype(np.float32)
    sin_signed_t = np.ascontiguousarray(sin.T * sign[:, None])
    return tuple(jnp.asarray(t) for t in (cos, sin, np.ascontiguousarray(cos.T), sin_signed_t))


def kernel(x, c, ctx, c_ctx, w_mod, b_mod, norm_g, w_in, mla_q_norm, mla_kv_norm, w_uq, w_ukv,
           hgrn_lb, hgrn_o_norm, w_br_mla, w_br_hgrn, w_out, w_ffn_in, w_ffn_out):
    bsz, length, d = x.shape
    ctx_len = ctx.shape[1]
    depth = w_mod.shape[0]
    assert depth == 1 and hgrn_lb.shape[0] == 2, "kernel is written for the depth-1 block"
    assert bsz < 8 and length % (HG_GROUP * HG_CHUNK) == 0 and ctx_len % HG_CHUNK == 0
    lk = length + ctx_len

    w16, w32 = _w_in_prep(jnp.swapaxes(w_in, 1, 2))

    wq = w_uq[0].reshape(Q_LORA, HEADS, QK_NOPE + QK_ROPE)
    wq_nope, wq_rope = wq[..., :QK_NOPE], wq[..., QK_NOPE:]
    zpad = jnp.zeros((Q_LORA, HEADS, QK_PAD - QK_NOPE - QK_ROPE), F32)
    wqm_t = jnp.concatenate([wq_nope, wq_rope, zpad], axis=-1).reshape(Q_LORA, HEADS * QK_PAD).T.astype(BF16)
    wkv = w_ukv[0].reshape(KV_LORA, HEADS, QK_NOPE + V_HEAD)
    wk = wkv[..., :QK_NOPE].reshape(KV_LORA, HEADS * QK_NOPE).astype(BF16)
    wvt = wkv[..., QK_NOPE:].reshape(KV_LORA, HEADS * V_HEAD).T.astype(BF16)
    w_bm = w_br_mla[0].astype(BF16)
    w_bh = w_br_hgrn[0].astype(BF16)
    w_o = w_out[0].astype(BF16)
    w_fi = w_ffn_in[0].astype(BF16)
    w_fo = w_ffn_out[0].astype(BF16)
    lb_rows = hgrn_lb.transpose(2, 0, 1, 3).reshape(HEADS, 4, HG_DIM)
    tables = _rope_tables(length, ctx_len)

    cc = jnp.concatenate([c, c_ctx[None, :], jnp.zeros((8 - bsz - 1, d), F32)], axis=0)
    mod = _mod(cc, w_mod[0], b_mod[0][None, :])
    ng = norm_g[0]

    hcat = _norm_mod(x, ctx, mod, ng)
    h2d = hcat.reshape(bsz * lk, d)
    p16 = _matmul(h2d, w16, BF16, "in_proj16").reshape(bsz, lk, N16)
    p32 = _matmul_strips(h2d, w32, "in_proj32")
    qt, k, vt = _mla_prep(p16, tables, mla_q_norm[0][None, :], mla_kv_norm[0][None, :],
                          wqm_t, wk, wvt, ctx_len)
    attn = _attention(qt, k, vt)
    hgn = _hgrn(p16, p32, lb_rows, hgrn_o_norm[0][None, :], ctx_len)
    x1, h2 = _merge(attn, hgn, p16, x, mod, ng, w_bm, w_bh, w_o)
    return _ffn(h2, x1, mod, ng, w_fi, w_fo)
```
